```python
import math
import jax, jax.numpy as jnp
from jax import lax
import numpy as np

D_MODEL = 4096
BATCH = 4
SEQ = 4096
DEPTH = 1

ATT_HEADS = 16
ATT_LAT = 256
ATT_VDIM = 128
IDX_HEADS = 16
IDX_DIM = 64
TOPK_MAX = 256
Q_BLOCK = 128
ML_HEADS = 4
ML_QK = 256
ML_V = 512
ML_CHUNK = 64
N_EXPERTS = 128
TOP_K = 8
N_GROUPS = 8
TOPK_GROUPS = 4
D_EXPERT = 384
D_SHARED = 384
ROUTED_SCALE = 2.5
MOE_BLOCK = 128
EPS = 1e-6

ATT_WIDTH = ATT_HEADS * ATT_VDIM
ML_WIDTH = ML_HEADS * ML_V
MIX_WIDTH = ATT_WIDTH + ML_WIDTH
SPLITS = (ATT_HEADS * ATT_LAT,
          ATT_LAT,
          IDX_HEADS * IDX_DIM,
          IDX_DIM,
          IDX_HEADS,
          ML_HEADS * ML_QK,
          ML_HEADS * ML_QK,
          ML_WIDTH,
          ML_WIDTH,
          ML_HEADS,
          ML_HEADS)
D_IN = sum(SPLITS)
SPLIT_POINTS = tuple(int(v) for v in np.cumsum(SPLITS)[:-1])

kernel_name = "hymba_dsa_mlstm_moe_adaln_block"

F32 = jnp.float32


def _rms(x):
    x32 = x.astype(F32)
    return x32 * lax.rsqrt(jnp.mean(x32 * x32, axis=-1, keepdims=True) + EPS)


def _alibi_slopes(n):
    return jnp.exp2(-8.0 * jnp.arange(1, n + 1, dtype=F32) / n)


def _dsa_attention(q, kv, q_idx, k_idx, w_idx, w_uv):
    B, S = q.shape[0], q.shape[1]
    k_sel = min(TOPK_MAX, S // 4)
    nblk = S // Q_BLOCK
    slopes = _alibi_slopes(ATT_HEADS)
    spos = jnp.arange(S, dtype=jnp.int32)

    def to_blocks(a):
        return jnp.swapaxes(a.reshape((B, nblk, Q_BLOCK) + a.shape[2:]), 0, 1)

    def block(args):
        start, qb, qib, wb = args
        tpos = start + jnp.arange(Q_BLOCK, dtype=jnp.int32)
        rel = jnp.einsum('bqhd,bsd->bqsh', qib, k_idx)
        score = jnp.einsum('bqsh,bqh->bqs', jax.nn.relu(rel), wb).astype(F32)
        score = jnp.where((spos[None, :] <= tpos[:, None])[None], score, -jnp.inf)
        _, sel = lax.top_k(score, k_sel)
        kv_sel = jax.vmap(lambda kvb, ib: kvb[ib])(kv, sel)
        logits = jnp.einsum('bqhd,bqkd->bqhk', qb, kv_sel).astype(F32)
        dist = (tpos[None, :, None] - sel).astype(F32)
        logits = logits - slopes[None, None, :, None] * dist[:, :, None, :]
        valid = (sel <= tpos[None, :, None])[:, :, None, :]
        p = jax.nn.softmax(jnp.where(valid, logits, -jnp.inf), axis=-1).astype(kv.dtype)
        o_lat = jnp.einsum('bqhk,bqkd->bqhd', p, kv_sel)
        return jnp.einsum('bqhd,hde->bqhe', o_lat, w_uv)

    starts = jnp.arange(nblk, dtype=jnp.int32) * Q_BLOCK
    out = lax.map(block, (starts, to_blocks(q), to_blocks(q_idx), to_blocks(w_idx)))
    return jnp.swapaxes(out, 0, 1).reshape(B, S, ATT_HEADS, ATT_VDIM)


def _mlstm(q, k, v, i_pre, f_pre):
    B, S, H, DK = q.shape
    DV = v.shape[-1]
    L = ML_CHUNK
    nc = S // L

    def chunks(a):
        a = a.astype(F32).reshape((B, nc, L, H) + a.shape[3:])
        return jnp.moveaxis(jnp.swapaxes(a, 2, 3), 1, 0)

    qc = chunks(q) * (DK ** -0.5)
    kc, vc = chunks(k), chunks(v)
    ic = chunks(i_pre.astype(F32))
    lfc = chunks(jax.nn.log_sigmoid(f_pre.astype(F32)))
    tril = jnp.tril(jnp.ones((L, L), dtype=bool))

    def step(carry, inp):
        C, n, m = carry
        qt, kt, vt, it, lft = inp
        b = jnp.cumsum(lft, axis=-1)
        dmat = jnp.where(tril, b[..., :, None] - b[..., None, :] + it[..., None, :], -jnp.inf)
        m_t = jnp.maximum(b + m[..., None], jnp.max(dmat, axis=-1))
        inter = jnp.exp(b + m[..., None] - m_t)
        s = jnp.einsum('bhtd,bhsd->bhts', qt, kt) * jnp.exp(dmat - m_t[..., None])
        num = inter[..., None] * jnp.einsum('bhtd,bhde->bhte', qt, C) + jnp.einsum('bhts,bhse->bhte', s, vt)
        den = inter * jnp.einsum('bhtd,bhd->bht', qt, n) + jnp.sum(s, axis=-1)
        h = num / jnp.maximum(jnp.abs(den), jnp.exp(-m_t))[..., None]
        bl = b[..., -1]
        g = bl[..., None] - b + it
        m_new = jnp.maximum(bl + m, jnp.max(g, axis=-1))
        decay = jnp.exp(bl + m - m_new)
        wk = jnp.exp(g - m_new[..., None])
        C_new = decay[..., None, None] * C + jnp.einsum('bhs,bhsd,bhse->bhde', wk, kt, vt)
        n_new = decay[..., None] * n + jnp.einsum('bhs,bhsd->bhd', wk, kt)
        return (C_new, n_new, m_new), h

    init = (jnp.zeros((B, H, DK, DV), F32), jnp.zeros((B, H, DK), F32), jnp.zeros((B, H), F32))
    _, hs = lax.scan(step, init, (qc, kc, vc, ic, lfc))
    return jnp.swapaxes(jnp.moveaxis(hs, 0, 1), 2, 3).reshape(B, S, H, DV)


def _mixer(h, w_in, q_norm_g, kv_norm_g, idx_k_norm_g, w_uv, gate_bias, attn_out_g, mlstm_out_g, w_out):
    B, S, _ = h.shape
    dt = h.dtype
    (q_lat, kv_lat, q_ix, k_ix, w_ix, q_m, k_m, v_m, o_m, i_m, f_m) = jnp.split(h @ w_in, SPLIT_POINTS, axis=-1)
    q = (_rms(q_lat.reshape(B, S, ATT_HEADS, ATT_LAT)) * q_norm_g * (ATT_LAT ** -0.5)).astype(dt)
    kv = (_rms(kv_lat) * kv_norm_g).astype(dt)
    q_ix = q_ix.reshape(B, S, IDX_HEADS, IDX_DIM) * (IDX_DIM ** -0.5)
    k_ix = (_rms(k_ix) * idx_k_norm_g).astype(dt)
    w_ix = w_ix * (IDX_HEADS ** -0.5)
    att = _dsa_attention(q, kv, q_ix, k_ix, w_ix, w_uv)
    att = _rms(att) * attn_out_g.reshape(ATT_HEADS, ATT_VDIM)
    ig = i_m.astype(F32) + gate_bias[:ML_HEADS]
    fg = f_m.astype(F32) + gate_bias[ML_HEADS:]
    hm = _mlstm(q_m.reshape(B, S, ML_HEADS, ML_QK), k_m.reshape(B, S, ML_HEADS, ML_QK),
                v_m.reshape(B, S, ML_HEADS, ML_V), ig, fg)
    hm = _rms(hm) * mlstm_out_g.reshape(ML_HEADS, ML_V) * jax.nn.sigmoid(o_m.astype(F32)).reshape(B, S, ML_HEADS, ML_V)
    y = jnp.concatenate([att.reshape(B, S, ATT_WIDTH), hm.reshape(B, S, ML_WIDTH)], axis=-1).astype(dt)
    return y @ w_out


def _moe(h, w_router, router_bias, w_gate_e, w_up_e, w_down_e, w_gate_s, w_up_s, w_down_s):
    B, S, D = h.shape
    n_tok = B * S
    dt = h.dtype
    xf = h.reshape(n_tok, D)
    scores = jax.nn.sigmoid(xf.astype(F32) @ w_router.astype(F32))
    biased = scores + router_bias.astype(F32)
    grp = biased.reshape(n_tok, N_GROUPS, N_EXPERTS // N_GROUPS)
    grp_score = jnp.sum(lax.top_k(grp, 2)[0], axis=-1)
    _, top_g = lax.top_k(grp_score, TOPK_GROUPS)
    gmask = jnp.sum(jax.nn.one_hot(top_g, N_GROUPS, dtype=F32), axis=1) > 0
    emask = jnp.repeat(gmask, N_EXPERTS // N_GROUPS, axis=-1)
    _, eidx = lax.top_k(jnp.where(emask, biased, -jnp.inf), TOP_K)
    gw = jnp.take_along_axis(scores, eidx, axis=-1)
    gw = gw / jnp.sum(gw, axis=-1, keepdims=True) * ROUTED_SCALE
    n_asg = n_tok * TOP_K
    flat_e = eidx.reshape(-1)
    flat_tok = jnp.repeat(jnp.arange(n_tok, dtype=jnp.int32), TOP_K)
    flat_w = gw.reshape(-1)
    order = jnp.argsort(flat_e)
    se, stok, sw = flat_e[order], flat_tok[order], flat_w[order]
    counts = jnp.bincount(flat_e, length=N_EXPERTS)
    padded = (counts + MOE_BLOCK - 1) // MOE_BLOCK * MOE_BLOCK
    pend = jnp.cumsum(padded)
    pstart = pend - padded
    sstart = jnp.cumsum(counts) - counts
    dest = pstart[se] + jnp.arange(n_asg, dtype=jnp.int32) - sstart[se]
    nblk = -(-n_asg // MOE_BLOCK) + N_EXPERTS
    n_rows = nblk * MOE_BLOCK
    row_tok = jnp.full((n_rows,), n_tok, jnp.int32).at[dest].set(stok)
    row_w = jnp.zeros((n_rows,), F32).at[dest].set(sw)
    blk_e = jnp.minimum(jnp.searchsorted(pend, jnp.arange(nblk, dtype=jnp.int32) * MOE_BLOCK, side='right'), N_EXPERTS - 1)
    x_pad = jnp.concatenate([xf, jnp.zeros((1, D), dt)], axis=0)

    def expert_block(y, inp):
        e, tok, w = inp
        xb = x_pad[tok]
        hb = jax.nn.silu(xb @ w_gate_e[e]) * (xb @ w_up_e[e])
        yb = ((hb @ w_down_e[e]).astype(F32) * w[:, None]).astype(dt)
        return y.at[tok].add(yb), None

    y, _ = lax.scan(expert_block, jnp.zeros((n_tok + 1, D), dt),
                    (blk_e, row_tok.reshape(nblk, MOE_BLOCK), row_w.reshape(nblk, MOE_BLOCK)))
    shared = (jax.nn.silu(xf @ w_gate_s) * (xf @ w_up_s)) @ w_down_s
    return (y[:n_tok] + shared).reshape(B, S, D)


def setup_inputs(seed: int = 0) -> dict:
    key = jax.random.key(seed)
    ks = jax.random.split(key, 24)
    L = DEPTH
    D = D_MODEL

    def nrm(k, shape, scale):
        return jax.random.normal(k, shape, F32) * scale

    gate_bias = jnp.concatenate([
        nrm(ks[8], (L, ML_HEADS), 0.1),
        jnp.linspace(3.0, 6.0, ML_HEADS, dtype=F32)[None, :] + nrm(ks[9], (L, ML_HEADS), 0.1)], axis=-1)
    return {
        "x": nrm(ks[0], (BATCH, SEQ, D), 1.0),
        "c": nrm(ks[1], (BATCH, D), 1.0),
        "w_ada": nrm(ks[2], (L, D, 6 * D), 0.5 * D ** -0.5),
        "b_ada": nrm(ks[3], (L, 6 * D), 0.02),
        "w_in": nrm(ks[4], (L, D, D_IN), D ** -0.5),
        "q_norm_g": 1.0 + nrm(ks[5], (L, ATT_LAT), 0.02),
        "kv_norm_g": 1.0 + nrm(ks[6], (L, ATT_LAT), 0.02),
        "idx_k_norm_g": 1.0 + nrm(ks[7], (L, IDX_DIM), 0.02),
        "w_uv": nrm(ks[10], (L, ATT_HEADS, ATT_LAT, ATT_VDIM), ATT_LAT ** -0.5),
        "mlstm_gate_bias": gate_bias,
        "attn_out_g": 1.0 + nrm(ks[11], (L, ATT_WIDTH), 0.02),
        "mlstm_out_g": 1.0 + nrm(ks[12], (L, ML_WIDTH), 0.02),
        "w_out": nrm(ks[13], (L, MIX_WIDTH, D), MIX_WIDTH ** -0.5),
        "w_router": nrm(ks[14], (L, D, N_EXPERTS), D ** -0.5),
        "router_bias": nrm(ks[15], (L, N_EXPERTS), 0.01),
        "w_gate_e": nrm(ks[16], (L, N_EXPERTS, D, D_EXPERT), D ** -0.5),
        "w_up_e": nrm(ks[17], (L, N_EXPERTS, D, D_EXPERT), D ** -0.5),
        "w_down_e": nrm(ks[18], (L, N_EXPERTS, D_EXPERT, D), D_EXPERT ** -0.5),
        "w_gate_s": nrm(ks[19], (L, D, D_SHARED), D ** -0.5),
        "w_up_s": nrm(ks[20], (L, D, D_SHARED), D ** -0.5),
        "w_down_s": nrm(ks[21], (L, D_SHARED, D), D_SHARED ** -0.5),
    }


def reference(x, c, w_ada, b_ada, w_in, q_norm_g, kv_norm_g, idx_k_norm_g, w_uv, mlstm_gate_bias,
              attn_out_g, mlstm_out_g, w_out, w_router, router_bias, w_gate_e, w_up_e, w_down_e,
              w_gate_s, w_up_s, w_down_s):
    dt = x.dtype
    cond = jax.nn.silu(c)
    for l in range(DEPTH):
        mod = (cond @ w_ada[l] + b_ada[l]).astype(F32)[:, None, :]
        sh1, sc1, g1, sh2, sc2, g2 = jnp.split(mod, 6, axis=-1)
        h = (_rms(x) * (1.0 + sc1) + sh1).astype(dt)
        mix = _mixer(h, w_in[l], q_norm_g[l], kv_norm_g[l], idx_k_norm_g[l], w_uv[l], mlstm_gate_bias[l],
                     attn_out_g[l], mlstm_out_g[l], w_out[l])
        x = x + (g1 * mix.astype(F32)).astype(dt)
        h = (_rms(x) * (1.0 + sc2) + sh2).astype(dt)
        ffn = _moe(h, w_router[l], router_bias[l], w_gate_e[l], w_up_e[l], w_down_e[l],
                   w_gate_s[l], w_up_s[l], w_down_s[l])
        x = x + (g2 * ffn.astype(F32)).astype(dt)
    return x
```

```python
import functools

import jax
import jax.numpy as jnp
import numpy as np
from jax import lax
from jax.experimental import pallas as pl
from jax.experimental.pallas import tpu as pltpu

F32 = jnp.float32
BF16 = jnp.bfloat16
I32 = jnp.int32

EPS = 1e-6
ATT_HEADS = 16
ATT_LAT = 256
ATT_VDIM = 128
IDX_HEADS = 16
IDX_DIM = 64
IDX_PAD = 128
TOPK_MAX = 256
ML_HEADS = 4
ML_QK = 256
ML_V = 512
N_EXPERTS = 128
TOP_K = 8
N_GROUPS = 8
TOPK_GROUPS = 4
GROUP_SIZE = N_EXPERTS // N_GROUPS
D_EXPERT = 384
ROUTED_SCALE = 2.5

ATT_WIDTH = ATT_HEADS * ATT_VDIM
ML_WIDTH = ML_HEADS * ML_V
SPLITS = (ATT_HEADS * ATT_LAT, ATT_LAT, IDX_HEADS * IDX_DIM, IDX_DIM, IDX_HEADS,
          ML_HEADS * ML_QK, ML_HEADS * ML_QK, ML_WIDTH, ML_WIDTH, ML_HEADS, ML_HEADS)
SPLIT_STARTS = tuple(int(v) for v in np.cumsum((0,) + SPLITS[:-1]))

SM_KIX = 0
SM_WIX = IDX_DIM
SM_IG = SM_WIX + IDX_HEADS
SM_FG = SM_IG + ML_HEADS
SM_WIDTH = 128

NEG_BIG = -1e30
INT_MIN = int(np.iinfo(np.int32).min)
V7X_VMEM_LIMIT = 56 * 1024 * 1024


def _cparams(sem, vmem_mb=None):
    kw = dict(dimension_semantics=sem)
    if vmem_mb is not None:
        kw["vmem_limit_bytes"] = min(vmem_mb * 1024 * 1024, V7X_VMEM_LIMIT)
    return pltpu.CompilerParams(**kw)


def _sigmoid(x):
    return 1.0 / (1.0 + jnp.exp(-x))


def _pick(n, pref):
    t = min(n, pref)
    while n % t:
        t //= 2
    return t


def _ada_kernel(c_ref, w_ref, b_ref, o_ref):
    c = c_ref[...]
    cond = (c * _sigmoid(c)).astype(BF16)
    o_ref[...] = jnp.dot(cond, w_ref[...].astype(BF16), preferred_element_type=F32) + b_ref[...]


def _ada_call(c8, w, b):
    rows, d = c8.shape
    nout = w.shape[1]
    tn = _pick(nout, 512)
    return pl.pallas_call(
        _ada_kernel,
        out_shape=jax.ShapeDtypeStruct((rows, nout), F32),
        grid=(nout // tn,),
        in_specs=[pl.BlockSpec((rows, d), lambda j: (0, 0)),
                  pl.BlockSpec((d, tn), lambda j: (0, j)),
                  pl.BlockSpec((1, tn), lambda j: (0, j))],
        out_specs=pl.BlockSpec((rows, tn), lambda j: (0, j)),
        compiler_params=_cparams(("arbitrary",), 40),
        name="ada_proj",
    )(c8, w, b)


def _modnorm_kernel(x_ref, sc_ref, sh_ref, o_ref):
    x = x_ref[...]
    r = x * lax.rsqrt(jnp.mean(x * x, axis=-1, keepdims=True) + EPS)
    o_ref[...] = (r * (1.0 + sc_ref[...]) + sh_ref[...]).astype(o_ref.dtype)


def _modnorm_call(x2, sc, sh, seq):
    n, d = x2.shape
    tm = _pick(seq, 256)
    per = seq // tm
    return pl.pallas_call(
        _modnorm_kernel,
        out_shape=jax.ShapeDtypeStruct((n, d), BF16),
        grid=(n // tm,),
        in_specs=[pl.BlockSpec((tm, d), lambda i: (i, 0)),
                  pl.BlockSpec((None, 1, d), lambda i: (i // per, 0, 0)),
                  pl.BlockSpec((None, 1, d), lambda i: (i // per, 0, 0))],
        out_specs=pl.BlockSpec((tm, d), lambda i: (i, 0)),
        compiler_params=_cparams(("arbitrary",), 32),
        name="modnorm",
    )(x2, sc, sh)


def _mm_kernel(a_ref, w_ref, *rest, epilogue):
    o_ref = rest[-1]
    acc = jnp.dot(a_ref[...], w_ref[...], preferred_element_type=F32)
    o_ref[...] = epilogue(acc, *[e[...] for e in rest[:-1]]).astype(o_ref.dtype)


def _mm_call(a, w, extras, epilogue, out_dtype, tm, tn, name):
    m, k = a.shape
    nw = w.shape[1]
    tm = _pick(m, tm)
    tn = _pick(nw, tn)
    in_specs = [pl.BlockSpec((tm, k), lambda i, j: (i, 0)),
                pl.BlockSpec((k, tn), lambda i, j: (0, j))]
    in_specs += [pl.BlockSpec((1, tn), lambda i, j: (0, j)) for _ in extras]
    return pl.pallas_call(
        functools.partial(_mm_kernel, epilogue=epilogue),
        out_shape=jax.ShapeDtypeStruct((m, nw), out_dtype),
        grid=(m // tm, nw // tn),
        in_specs=in_specs,
        out_specs=pl.BlockSpec((tm, tn), lambda i, j: (i, j)),
        compiler_params=_cparams(("arbitrary", "arbitrary"), 48),
        name=name,
    )(a, w, *extras)


def _ep_none(acc):
    return acc


def _ep_colscale(acc, cs):
    return acc * cs


def _ep_grouprms(acc, g, *, width, scale):
    outs = []
    for s in range(0, acc.shape[1], width):
        a = acc[:, s:s + width]
        r = a * lax.rsqrt(jnp.mean(a * a, axis=-1, keepdims=True) + EPS) * g[:, s:s + width]
        outs.append(r * scale if scale is not None else r)
    return outs[0] if len(outs) == 1 else jnp.concatenate(outs, axis=1)


def _ep_small(acc, g, bias):
    lane = lax.broadcasted_iota(I32, acc.shape, 1)
    is_k = lane < SM_WIX
    ms = jnp.sum(jnp.where(is_k, acc * acc, 0.0), axis=-1, keepdims=True) * (1.0 / IDX_DIM)
    kix = acc * lax.rsqrt(ms + EPS) * g
    return jnp.where(is_k, kix, jnp.where(lane < SM_IG, acc * (IDX_HEADS ** -0.5), acc + bias))


def _attn_kernel(slopes_ref, q_ref, qix_ref, small_ref, kixt_ref, kvt_ref, kv_ref, wuv_ref, go_ref,
                 o_ref, key_sc, qs_sc, lg_sc, p_sc, acc_sc, m_sc, l_sc, *, tq, tk, k_sel):
    i = pl.program_id(1)
    q0 = i * tq
    nchunks = (q0 + tq + tk - 1) // tk
    for h in range(ATT_HEADS):
        qs_sc[h * tq:(h + 1) * tq, :] = q_ref[:, h * ATT_LAT:(h + 1) * ATT_LAT]
    row_t = q0 + lax.broadcasted_iota(I32, (tq, tk), 0)
    col_i = lax.broadcasted_iota(I32, (tq, tk), 1)
    wcols = [small_ref[:, SM_WIX + h:SM_WIX + h + 1] for h in range(IDX_HEADS)]

    def score_chunk(c, carry):
        off = pl.multiple_of(c * tk, tk)
        kc = kixt_ref[:, pl.ds(off, tk)]
        acc = jnp.zeros((tq, tk), F32)
        for h in range(IDX_HEADS):
            rel = jnp.dot(qix_ref[:, h * IDX_PAD:(h + 1) * IDX_PAD], kc, preferred_element_type=F32)
            acc = acc + jnp.maximum(rel, 0.0) * wcols[h]
        bits = pltpu.bitcast(acc, I32)
        key = bits ^ ((bits >> 31) & 0x7FFFFFFF)
        key_sc[:, pl.ds(off, tk)] = jnp.where(off + col_i <= row_t, key, INT_MIN)
        return carry

    lax.fori_loop(0, nchunks, score_chunk, 0)

    def count_ge(cand):
        def body(c, cnt):
            off = pl.multiple_of(c * tk, tk)
            ge = (key_sc[:, pl.ds(off, tk)] >= cand).astype(I32)
            part = ge[:, 0:128]
            for s in range(128, tk, 128):
                part = part + ge[:, s:s + 128]
            return cnt + part
        cnt = lax.fori_loop(0, nchunks, body, jnp.zeros((tq, 128), I32))
        return jnp.sum(cnt, axis=-1, keepdims=True)

    def bisect(it, t_u):
        cand_u = t_u | (jnp.int32(1) << (31 - it))
        cnt = count_ge(cand_u ^ INT_MIN)
        return jnp.where(cnt >= k_sel, cand_u, t_u)

    t_u = lax.fori_loop(0, 32, bisect, jnp.zeros((tq, 1), I32))
    thr = jnp.maximum(t_u ^ INT_MIN, INT_MIN + 1)

    m_sc[...] = jnp.full(m_sc.shape, NEG_BIG, F32)
    l_sc[...] = jnp.zeros(l_sc.shape, F32)
    acc_sc[...] = jnp.zeros(acc_sc.shape, F32)

    def att_chunk(c, carry):
        off = pl.multiple_of(c * tk, tk)
        madd = jnp.where(key_sc[:, pl.ds(off, tk)] >= thr, 0.0, NEG_BIG)
        dist = (row_t - (off + col_i)).astype(F32)
        lg_sc[...] = jnp.dot(qs_sc[...], kvt_ref[:, pl.ds(off, tk)], preferred_element_type=F32)

        def head(h, hc):
            rows = pl.ds(pl.multiple_of(h * tq, tq), tq)
            lg = lg_sc[rows, :] - slopes_ref[h] * dist + madd
            m_old = m_sc[rows, :]
            m_new = jnp.maximum(m_old, jnp.max(lg, axis=-1, keepdims=True))
            p = jnp.exp(lg - m_new)
            alpha = jnp.exp(m_old - m_new)
            l_sc[rows, :] = alpha * l_sc[rows, :] + jnp.sum(p, axis=-1, keepdims=True)
            m_sc[rows, :] = m_new
            p_sc[rows, :] = p.astype(BF16)
            acc_sc[rows, :] = alpha * acc_sc[rows, :]
            return hc

        lax.fori_loop(0, ATT_HEADS, head, 0)
        acc_sc[...] += jnp.dot(p_sc[...], kv_ref[pl.ds(off, tk), :], preferred_element_type=F32)
        return carry

    lax.fori_loop(0, nchunks, att_chunk, 0)

    for h in range(ATT_HEADS):
        o_lat = acc_sc[h * tq:(h + 1) * tq, :] / l_sc[h * tq:(h + 1) * tq, :]
        a = jnp.dot(o_lat.astype(BF16), wuv_ref[h], preferred_element_type=F32)
        r = a * lax.rsqrt(jnp.mean(a * a, axis=-1, keepdims=True) + EPS)
        r = r * go_ref[:, h * ATT_VDIM:(h + 1) * ATT_VDIM]
        o_ref[:, h * ATT_VDIM:(h + 1) * ATT_VDIM] = r.astype(o_ref.dtype)


def _attn_call(slopes, q, qix, qix_col, small, kixt, kvt, kv, wuv, go, batch, seq):
    tq = _pick(seq, 256)
    tk = tq
    k_sel = min(TOPK_MAX, seq // 4)
    nq = seq // tq
    hq = ATT_HEADS * tq
    kern = functools.partial(_attn_kernel, tq=tq, tk=tk, k_sel=k_sel)
    return pl.pallas_call(
        kern,
        out_shape=jax.ShapeDtypeStruct((batch * seq, ATT_WIDTH), BF16),
        grid=(batch, nq),
        in_specs=[pl.BlockSpec(memory_space=pltpu.SMEM),
                  pl.BlockSpec((tq, ATT_HEADS * ATT_LAT), lambda b, i: (b * nq + i, 0)),
                  pl.BlockSpec((tq, IDX_HEADS * IDX_PAD), lambda b, i: (b * nq + i, qix_col)),
                  pl.BlockSpec((tq, SM_WIDTH), lambda b, i: (b * nq + i, 0)),
                  pl.BlockSpec((None, IDX_PAD, seq), lambda b, i: (b, 0, 0)),
                  pl.BlockSpec((None, ATT_LAT, seq), lambda b, i: (b, 0, 0)),
                  pl.BlockSpec((seq, ATT_LAT), lambda b, i: (b, 0)),
                  pl.BlockSpec((ATT_HEADS, ATT_LAT, ATT_VDIM), lambda b, i: (0, 0, 0)),
                  pl.BlockSpec((1, ATT_WIDTH), lambda b, i: (0, 0))],
        out_specs=pl.BlockSpec((tq, ATT_WIDTH), lambda b, i: (b * nq + i, 0)),
        scratch_shapes=[pltpu.VMEM((tq, seq), I32),
                        pltpu.VMEM((hq, ATT_LAT), BF16),
                        pltpu.VMEM((hq, tk), F32),
                        pltpu.VMEM((hq, tk), BF16),
                        pltpu.VMEM((hq, ATT_LAT), F32),
                        pltpu.VMEM((hq, 1), F32),
                        pltpu.VMEM((hq, 1), F32)],
        compiler_params=_cparams(("arbitrary", "arbitrary"), 56),
        name="dsa_attention",
    )(slopes, q, qix, small, kixt, kvt, kv, wuv, go)


def _log_sigmoid(x):
    return jnp.minimum(x, 0.0) - jnp.log1p(jnp.exp(-jnp.abs(x)))


def _mlstm_kernel(q_ref, k_ref, v_ref, og_ref, gcol_ref, grow_ref, g_ref, o_ref, c_sc, n_sc, m_sc, *, lc):
    @pl.when(pl.program_id(1) == 0)
    def _():
        c_sc[...] = jnp.zeros(c_sc.shape, F32)
        n_sc[...] = jnp.zeros(n_sc.shape, F32)
        m_sc[...] = jnp.zeros(m_sc.shape, F32)

    ti = lax.broadcasted_iota(I32, (lc, lc), 0)
    si = lax.broadcasted_iota(I32, (lc, lc), 1)
    tril = si <= ti
    for h in range(ML_HEADS):
        ig_col = gcol_ref[:, SM_IG + h:SM_IG + h + 1]
        lf_col = _log_sigmoid(gcol_ref[:, SM_FG + h:SM_FG + h + 1])
        ig_row = grow_ref[h:h + 1, :]
        lf_row = _log_sigmoid(grow_ref[ML_HEADS + h:ML_HEADS + h + 1, :])
        b_col = jnp.sum(jnp.where(tril, lf_row, 0.0), axis=1, keepdims=True)
        b_row = jnp.sum(jnp.where(ti <= si, lf_col, 0.0), axis=0, keepdims=True)
        m_prev = m_sc[h][:, 0:1]
        dmat = jnp.where(tril, b_col - b_row + ig_row, NEG_BIG)
        m_t = jnp.maximum(b_col + m_prev, jnp.max(dmat, axis=1, keepdims=True))
        inter = jnp.exp(b_col + m_prev - m_t)
        qh = q_ref[:, h * ML_QK:(h + 1) * ML_QK]
        kh = k_ref[:, h * ML_QK:(h + 1) * ML_QK]
        vh = v_ref[:, h * ML_V:(h + 1) * ML_V]
        qk = lax.dot_general(qh, kh, (((1,), (1,)), ((), ())), preferred_element_type=F32)
        s = qk * jnp.exp(dmat - m_t)
        c_old = c_sc[h]
        num = inter * jnp.dot(qh, c_old.astype(BF16), preferred_element_type=F32)
        num = num + jnp.dot(s.astype(BF16), vh, preferred_element_type=F32)
        qn = jnp.sum(qh.astype(F32) * n_sc[h], axis=1, keepdims=True)
        den = inter * qn + jnp.sum(s, axis=1, keepdims=True)
        hval = num / jnp.maximum(jnp.abs(den), jnp.exp(-m_t))
        bl = b_col[lc - 1:lc, :]
        g_col = bl - b_col + ig_col
        m_new = jnp.maximum(bl + m_prev, jnp.max(g_col, axis=0, keepdims=True))
        decay = jnp.exp(bl + m_prev - m_new)
        kw = kh.astype(F32) * jnp.exp(g_col - m_new)
        c_sc[h] = decay * c_old + jnp.dot(kw.T.astype(BF16), vh, preferred_element_type=F32)
        n_sc[h] = decay * n_sc[h] + jnp.sum(kw, axis=0, keepdims=True)
        m_sc[h] = jnp.broadcast_to(m_new, (1, 128))
        r = hval * lax.rsqrt(jnp.mean(hval * hval, axis=-1, keepdims=True) + EPS)
        r = r * g_ref[:, h * ML_V:(h + 1) * ML_V] * _sigmoid(og_ref[:, h * ML_V:(h + 1) * ML_V])
        o_ref[:, h * ML_V:(h + 1) * ML_V] = r.astype(o_ref.dtype)


def _mlstm_call(proj, qcol, kcol, vcol, ogate, small, grow, gain, batch, seq):
    lc = _pick(seq, 256)
    nc = seq // lc
    qw = ML_HEADS * ML_QK
    return pl.pallas_call(
        functools.partial(_mlstm_kernel, lc=lc),
        out_shape=jax.ShapeDtypeStruct((batch * seq, ML_WIDTH), BF16),
        grid=(batch, nc),
        in_specs=[pl.BlockSpec((lc, qw), lambda b, c: (b * nc + c, qcol)),
                  pl.BlockSpec((lc, qw), lambda b, c: (b * nc + c, kcol)),
                  pl.BlockSpec((lc, ML_WIDTH), lambda b, c: (b * nc + c, vcol)),
                  pl.BlockSpec((lc, ML_WIDTH), lambda b, c: (b * nc + c, 0)),
                  pl.BlockSpec((lc, SM_WIDTH), lambda b, c: (b * nc + c, 0)),
                  pl.BlockSpec((None, 2 * ML_HEADS, lc), lambda b, c: (b, 0, c)),
                  pl.BlockSpec((1, ML_WIDTH), lambda b, c: (0, 0))],
        out_specs=pl.BlockSpec((lc, ML_WIDTH), lambda b, c: (b * nc + c, 0)),
        scratch_shapes=[pltpu.VMEM((ML_HEADS, ML_QK, ML_V), F32),
                        pltpu.VMEM((ML_HEADS, 1, ML_QK), F32),
                        pltpu.VMEM((ML_HEADS, 1, 128), F32)],
        compiler_params=_cparams(("arbitrary", "arbitrary"), 40),
        name="mlstm",
    )(proj, proj, proj, ogate, small, grow, gain)


def _outproj_kernel(ya_ref, yb_ref, wa_ref, wb_ref, x_ref, g_ref, o_ref):
    acc = jnp.dot(ya_ref[...], wa_ref[...], preferred_element_type=F32)
    acc = acc + jnp.dot(yb_ref[...], wb_ref[...], preferred_element_type=F32)
    o_ref[...] = x_ref[...] + g_ref[...] * acc


def _outproj_call(ya, yb, w, x2, gate, seq):
    n, d = x2.shape
    ka, kb = ya.shape[1], yb.shape[1]
    assert ka == kb
    tm = _pick(seq, 512)
    tn = _pick(d, 512)
    per = seq // tm
    return pl.pallas_call(
        _outproj_kernel,
        out_shape=jax.ShapeDtypeStruct((n, d), F32),
        grid=(n // tm, d // tn),
        in_specs=[pl.BlockSpec((tm, ka), lambda i, j: (i, 0)),
                  pl.BlockSpec((tm, kb), lambda i, j: (i, 0)),
                  pl.BlockSpec((ka, tn), lambda i, j: (0, j)),
                  pl.BlockSpec((kb, tn), lambda i, j: (1, j)),
                  pl.BlockSpec((tm, tn), lambda i, j: (i, j)),
                  pl.BlockSpec((None, 1, tn), lambda i, j: (i // per, 0, j))],
        out_specs=pl.BlockSpec((tm, tn), lambda i, j: (i, j)),
        compiler_params=_cparams(("arbitrary", "arbitrary"), 40),
        name="out_proj",
    )(ya, yb, w, w, x2, gate)


def _router_kernel(h_ref, w_ref, bias_ref, idx_ref, gw_ref):
    logits = jnp.dot(h_ref[...], w_ref[...], preferred_element_type=F32)
    scores = _sigmoid(logits)
    biased = scores + bias_ref[...]
    shape = biased.shape
    lane = lax.broadcasted_iota(I32, shape, 1)
    grp = lane // GROUP_SIZE
    ninf = -jnp.inf

    def first_max(v):
        mx = jnp.max(v, axis=-1, keepdims=True)
        ix = jnp.min(jnp.where(v == mx, lane, N_EXPERTS), axis=-1, keepdims=True)
        return mx, ix

    gscore = jnp.full(shape, ninf, F32)
    for g in range(N_GROUPS):
        vg = jnp.where(grp == g, biased, ninf)
        m1, i1 = first_max(vg)
        m2 = jnp.max(jnp.where(lane == i1, ninf, vg), axis=-1, keepdims=True)
        gscore = jnp.where(lane == g, m1 + m2, gscore)
    emask = jnp.zeros(shape, I32)
    for _ in range(TOPK_GROUPS):
        _, ig = first_max(gscore)
        emask = jnp.where(grp == ig, 1, emask)
        gscore = jnp.where(lane == ig, ninf, gscore)
    masked = jnp.where(emask > 0, biased, ninf)
    idx_out = jnp.zeros(shape, I32)
    w_out = jnp.zeros(shape, F32)
    wsum = jnp.zeros((shape[0], 1), F32)
    for k in range(TOP_K):
        _, ik = first_max(masked)
        sel = lane == ik
        wk = jnp.sum(jnp.where(sel, scores, 0.0), axis=-1, keepdims=True)
        idx_out = jnp.where(lane == k, ik, idx_out)
        w_out = jnp.where(lane == k, wk, w_out)
        wsum = wsum + wk
        masked = jnp.where(sel, ninf, masked)
    idx_ref[...] = idx_out
    gw_ref[...] = w_out / wsum * ROUTED_SCALE


def _router_call(h, w, bias):
    n, d = h.shape
    tm = _pick(n, 512)
    return pl.pallas_call(
        _router_kernel,
        out_shape=(jax.ShapeDtypeStruct((n, N_EXPERTS), I32), jax.ShapeDtypeStruct((n, N_EXPERTS), F32)),
        grid=(n // tm,),
        in_specs=[pl.BlockSpec((tm, d), lambda i: (i, 0)),
                  pl.BlockSpec((d, N_EXPERTS), lambda i: (0, 0)),
                  pl.BlockSpec((1, N_EXPERTS), lambda i: (0, 0))],
        out_specs=(pl.BlockSpec((tm, N_EXPERTS), lambda i: (i, 0)),
                   pl.BlockSpec((tm, N_EXPERTS), lambda i: (i, 0))),
        compiler_params=_cparams(("arbitrary",), 32),
        name="moe_router",
    )(h, w, bias)


def _swiglu(x, wg, wu, wd):
    g = jnp.dot(x, wg, preferred_element_type=F32)
    u = jnp.dot(x, wu, preferred_element_type=F32)
    hb = (g * _sigmoid(g)) * u
    return jnp.dot(hb.astype(BF16), wd, preferred_element_type=F32)


def _expert_kernel(be_ref, nused_ref, x_ref, wg_ref, wu_ref, wd_ref, rw_ref, o_ref):
    i = pl.program_id(0)

    @pl.when(i < nused_ref[0])
    def _():
        o_ref[...] = _swiglu(x_ref[...], wg_ref[...], wu_ref[...], wd_ref[...]) * rw_ref[...]

    @pl.when(i >= nused_ref[0])
    def _():
        o_ref[...] = jnp.zeros(o_ref.shape, o_ref.dtype)


def _expert_call(blk_e, nused, xs, wg, wu, wd, row_w, tm):
    n_rows, d = xs.shape
    de = wg.shape[2]
    grid_spec = pltpu.PrefetchScalarGridSpec(
        num_scalar_prefetch=2,
        grid=(n_rows // tm,),
        in_specs=[pl.BlockSpec((tm, d), lambda i, be, nu: (i, 0)),
                  pl.BlockSpec((None, d, de), lambda i, be, nu: (be[i], 0, 0)),
                  pl.BlockSpec((None, d, de), lambda i, be, nu: (be[i], 0, 0)),
                  pl.BlockSpec((None, de, d), lambda i, be, nu: (be[i], 0, 0)),
                  pl.BlockSpec((tm, 1), lambda i, be, nu: (i, 0))],
        out_specs=pl.BlockSpec((tm, d), lambda i, be, nu: (i, 0)),
    )
    return pl.pallas_call(
        _expert_kernel,
        out_shape=jax.ShapeDtypeStruct((n_rows, d), F32),
        grid_spec=grid_spec,
        compiler_params=_cparams(("arbitrary",), 52),
        name="moe_experts",
    )(blk_e, nused, xs, wg, wu, wd, row_w)


def _final_kernel(h_ref, r_ref, x_ref, g_ref, wg_ref, wu_ref, wd_ref, o_ref):
    shared = _swiglu(h_ref[...], wg_ref[...], wu_ref[...], wd_ref[...])
    o_ref[...] = x_ref[...] + g_ref[...] * (r_ref[...] + shared)


def _final_call(h, routed, x2, gate, wg, wu, wd, seq):
    n, d = x2.shape
    de = wg.shape[1]
    tm = _pick(seq, 128)
    per = seq // tm
    return pl.pallas_call(
        _final_kernel,
        out_shape=jax.ShapeDtypeStruct((n, d), F32),
        grid=(n // tm,),
        in_specs=[pl.BlockSpec((tm, d), lambda i: (i, 0)),
                  pl.BlockSpec((tm, d), lambda i: (i, 0)),
                  pl.BlockSpec((tm, d), lambda i: (i, 0)),
                  pl.BlockSpec((None, 1, d), lambda i: (i // per, 0, 0)),
                  pl.BlockSpec((d, de), lambda i: (0, 0)),
                  pl.BlockSpec((d, de), lambda i: (0, 0)),
                  pl.BlockSpec((de, d), lambda i: (0, 0))],
        out_specs=pl.BlockSpec((tm, d), lambda i: (i, 0)),
        compiler_params=_cparams(("arbitrary",), 48),
        name="shared_final",
    )(h, routed, x2, gate, wg, wu, wd)


def _mixer(h, w_in, q_norm_g, kv_norm_g, idx_k_norm_g, w_uv, gate_bias, attn_out_g, mlstm_out_g, batch, seq):
    n, d = h.shape
    st = SPLIT_STARTS

    def cols(i):
        return w_in[:, st[i]:st[i] + SPLITS[i]]

    w_q = cols(0).astype(BF16)
    w_kv = cols(1).astype(BF16)
    w_qix = jnp.pad(cols(2).reshape(d, IDX_HEADS, IDX_DIM), ((0, 0), (0, 0), (0, IDX_PAD - IDX_DIM)))
    w_proj = jnp.concatenate([cols(7), w_qix.reshape(d, IDX_HEADS * IDX_PAD), cols(5), cols(6)], axis=1).astype(BF16)
    w_og = cols(8).astype(BF16)
    n_small = SPLITS[3] + SPLITS[4] + SPLITS[9] + SPLITS[10]
    w_small = jnp.concatenate([cols(3), cols(4), cols(9), cols(10), jnp.zeros((d, SM_WIDTH - n_small), F32)],
                              axis=1).astype(BF16)
    qix_w = IDX_HEADS * IDX_PAD
    qw = ML_HEADS * ML_QK
    cscale = jnp.concatenate([jnp.ones((ML_WIDTH,), F32), jnp.full((qix_w,), IDX_DIM ** -0.5, F32),
                              jnp.full((qw,), ML_QK ** -0.5, F32), jnp.ones((qw,), F32)])[None, :]
    g_small = jnp.concatenate([idx_k_norm_g, jnp.zeros((SM_WIDTH - IDX_DIM,), F32)])[None, :]
    b_small = jnp.concatenate([jnp.zeros((SM_IG,), F32), gate_bias, jnp.zeros((SM_WIDTH - SM_IG - 2 * ML_HEADS,), F32)])[None, :]

    q = _mm_call(h, w_q, [jnp.tile(q_norm_g, ATT_HEADS)[None, :]],
                 functools.partial(_ep_grouprms, width=ATT_LAT, scale=ATT_LAT ** -0.5), BF16, 512, 512, "proj_q")
    kv = _mm_call(h, w_kv, [kv_norm_g[None, :]],
                  functools.partial(_ep_grouprms, width=ATT_LAT, scale=None), BF16, 512, 256, "proj_kv")
    proj = _mm_call(h, w_proj, [cscale], _ep_colscale, BF16, 512, 512, "proj_main")
    ogate = _mm_call(h, w_og, [], _ep_none, F32, 512, 512, "proj_ogate")
    small = _mm_call(h, w_small, [g_small, b_small], _ep_small, F32, 512, SM_WIDTH, "proj_small")

    kixt = jnp.swapaxes(small[:, :IDX_DIM].reshape(batch, seq, IDX_DIM), 1, 2).astype(BF16)
    kixt = jnp.pad(kixt, ((0, 0), (0, IDX_PAD - IDX_DIM), (0, 0)))
    kvt = jnp.swapaxes(kv.reshape(batch, seq, ATT_LAT), 1, 2)
    grow = jnp.swapaxes(small[:, SM_IG:SM_IG + 2 * ML_HEADS].reshape(batch, seq, 2 * ML_HEADS), 1, 2)
    slopes = jnp.exp2(-8.0 * jnp.arange(1, ATT_HEADS + 1, dtype=F32) / ATT_HEADS)

    assert ML_WIDTH % qix_w == 0 and (ML_WIDTH + qix_w) % qw == 0
    qix_col = ML_WIDTH // qix_w
    q_col = (ML_WIDTH + qix_w) // qw
    att = _attn_call(slopes, q, proj, qix_col, small, kixt, kvt, kv, w_uv.astype(BF16), attn_out_g[None, :],
                     batch, seq)
    hm = _mlstm_call(proj, q_col, q_col + 1, 0, ogate, small, grow, mlstm_out_g[None, :], batch, seq)
    return att, hm


def _moe_routed(h, w_router, router_bias, w_gate_e, w_up_e, w_down_e):
    n_tok, d = h.shape
    idx, gw = _router_call(h, w_router.astype(BF16), router_bias[None, :])
    eidx, gw = idx[:, :TOP_K], gw[:, :TOP_K]
    tm = 256 if n_tok * TOP_K >= 256 * N_EXPERTS else 8
    n_asg = n_tok * TOP_K
    flat_e = eidx.reshape(-1)
    flat_tok = jnp.repeat(jnp.arange(n_tok, dtype=I32), TOP_K)
    order = jnp.argsort(flat_e)
    se, stok, sw = flat_e[order], flat_tok[order], gw.reshape(-1)[order]
    counts = jnp.bincount(flat_e, length=N_EXPERTS)
    padded = (counts + tm - 1) // tm * tm
    pend = jnp.cumsum(padded)
    pstart = pend - padded
    sstart = jnp.cumsum(counts) - counts
    dest = (pstart[se] + jnp.arange(n_asg, dtype=I32) - sstart[se]).astype(I32)
    nblk = -(-n_asg // tm) + N_EXPERTS
    n_rows = nblk * tm
    row_tok = jnp.full((n_rows,), n_tok, I32).at[dest].set(stok)
    row_w = jnp.zeros((n_rows,), F32).at[dest].set(sw)
    blk_e = jnp.minimum(jnp.searchsorted(pend, jnp.arange(nblk, dtype=I32) * tm, side='right'),
                        N_EXPERTS - 1).astype(I32)
    nused = (pend[-1] // tm).astype(I32).reshape(1)
    xs = jnp.concatenate([h, jnp.zeros((1, d), h.dtype)], axis=0)[row_tok]
    ys = _expert_call(blk_e, nused, xs, w_gate_e.astype(BF16), w_up_e.astype(BF16), w_down_e.astype(BF16),
                      row_w[:, None], tm)
    pos = jnp.zeros((n_asg,), I32).at[order].set(dest)
    return ys[pos].reshape(n_tok, TOP_K, d).sum(axis=1)


def kernel(x, c, w_ada, b_ada, w_in, q_norm_g, kv_norm_g, idx_k_norm_g, w_uv, mlstm_gate_bias, attn_out_g,
           mlstm_out_g, w_out, w_router, router_bias, w_gate_e, w_up_e, w_down_e, w_gate_s, w_up_s, w_down_s):
    batch, seq, d = x.shape
    n = batch * seq
    x2 = x.reshape(n, d)
    c8 = jnp.pad(c, ((0, -batch % 8), (0, 0)))
    for l in range(w_ada.shape[0]):
        mod = _ada_call(c8, w_ada[l], b_ada[l][None, :])[:batch]
        sh1, sc1, g1, sh2, sc2, g2 = [m[:, None, :] for m in jnp.split(mod, 6, axis=-1)]
        h = _modnorm_call(x2, sc1, sh1, seq)
        att, hm = _mixer(h, w_in[l], q_norm_g[l], kv_norm_g[l], idx_k_norm_g[l], w_uv[l], mlstm_gate_bias[l],
                         attn_out_g[l], mlstm_out_g[l], batch, seq)
        x2 = _outproj_call(att, hm, w_out[l].astype(BF16), x2, g1, seq)
        h = _modnorm_call(x2, sc2, sh2, seq)
        routed = _moe_routed(h, w_router[l], router_bias[l], w_gate_e[l], w_up_e[l], w_down_e[l])
        x2 = _final_call(h, routed, x2, g2, w_gate_s[l].astype(BF16), w_up_s[l].astype(BF16),
                         w_down_s[l].astype(BF16), seq)
    return x2.reshape(batch, seq, d)
```

```python
import functools

import jax
import jax.numpy as jnp
import numpy as np
from jax import lax
from jax.experimental import pallas as pl
from jax.experimental.pallas import tpu as pltpu

F32 = jnp.float32
BF16 = jnp.bfloat16
I32 = jnp.int32

EPS = 1e-6
ATT_HEADS = 16
ATT_LAT = 256
ATT_VDIM = 128
IDX_HEADS = 16
IDX_DIM = 64
IDX_PAD = 128
TOPK_MAX = 256
ML_HEADS = 4
ML_QK = 256
ML_V = 512
N_EXPERTS = 128
TOP_K = 8
N_GROUPS = 8
TOPK_GROUPS = 4
GROUP_SIZE = N_EXPERTS // N_GROUPS
D_EXPERT = 384
ROUTED_SCALE = 2.5

ATT_WIDTH = ATT_HEADS * ATT_VDIM
ML_WIDTH = ML_HEADS * ML_V
SPLITS = (ATT_HEADS * ATT_LAT, ATT_LAT, IDX_HEADS * IDX_DIM, IDX_DIM, IDX_HEADS,
          ML_HEADS * ML_QK, ML_HEADS * ML_QK, ML_WIDTH, ML_WIDTH, ML_HEADS, ML_HEADS)
SPLIT_STARTS = tuple(int(v) for v in np.cumsum((0,) + SPLITS[:-1]))

SM_KIX = 0
SM_WIX = IDX_DIM
SM_IG = SM_WIX + IDX_HEADS
SM_FG = SM_IG + ML_HEADS
SM_WIDTH = 128

NEG_BIG = -1e30
INT_MIN = int(np.iinfo(np.int32).min)
V7X_VMEM_LIMIT = 56 * 1024 * 1024


def _cparams(sem, vmem_mb=None):
    kw = dict(dimension_semantics=sem)
    if vmem_mb is not None:
        kw["vmem_limit_bytes"] = min(vmem_mb * 1024 * 1024, V7X_VMEM_LIMIT)
    return pltpu.CompilerParams(**kw)


def _sigmoid(x):
    return 1.0 / (1.0 + jnp.exp(-x))


def _pick(n, pref):
    t = min(n, pref)
    while n % t:
        t //= 2
    return t


def _ada_kernel(c_ref, w_ref, b_ref, o_ref):
    c = c_ref[...]
    cond = (c * _sigmoid(c)).astype(BF16)
    o_ref[...] = jnp.dot(cond, w_ref[...].astype(BF16), preferred_element_type=F32) + b_ref[...]


def _ada_call(c8, w, b):
    rows, d = c8.shape
    nout = w.shape[1]
    tn = _pick(nout, 512)
    return pl.pallas_call(
        _ada_kernel,
        out_shape=jax.ShapeDtypeStruct((rows, nout), F32),
        grid=(nout // tn,),
        in_specs=[pl.BlockSpec((rows, d), lambda j: (0, 0)),
                  pl.BlockSpec((d, tn), lambda j: (0, j)),
                  pl.BlockSpec((1, tn), lambda j: (0, j))],
        out_specs=pl.BlockSpec((rows, tn), lambda j: (0, j)),
        compiler_params=_cparams(("arbitrary",), 40),
        name="ada_proj",
    )(c8, w, b)


def _modnorm_kernel(x_ref, sc_ref, sh_ref, o_ref):
    x = x_ref[...]
    r = x * lax.rsqrt(jnp.mean(x * x, axis=-1, keepdims=True) + EPS)
    o_ref[...] = (r * (1.0 + sc_ref[...]) + sh_ref[...]).astype(o_ref.dtype)


def _modnorm_call(x2, sc, sh, seq):
    n, d = x2.shape
    tm = _pick(seq, 256)
    per = seq // tm
    return pl.pallas_call(
        _modnorm_kernel,
        out_shape=jax.ShapeDtypeStruct((n, d), BF16),
        grid=(n // tm,),
        in_specs=[pl.BlockSpec((tm, d), lambda i: (i, 0)),
                  pl.BlockSpec((None, 1, d), lambda i: (i // per, 0, 0)),
                  pl.BlockSpec((None, 1, d), lambda i: (i // per, 0, 0))],
        out_specs=pl.BlockSpec((tm, d), lambda i: (i, 0)),
        compiler_params=_cparams(("arbitrary",), 32),
        name="modnorm",
    )(x2, sc, sh)


def _mm_kernel(a_ref, w_ref, *rest, epilogue):
    o_ref = rest[-1]
    acc = jnp.dot(a_ref[...], w_ref[...], preferred_element_type=F32)
    o_ref[...] = epilogue(acc, *[e[...] for e in rest[:-1]]).astype(o_ref.dtype)


def _mm_call(a, w, extras, epilogue, out_dtype, tm, tn, name):
    m, k = a.shape
    nw = w.shape[1]
    tm = _pick(m, tm)
    tn = _pick(nw, tn)
    in_specs = [pl.BlockSpec((tm, k), lambda i, j: (i, 0)),
                pl.BlockSpec((k, tn), lambda i, j: (0, j))]
    in_specs += [pl.BlockSpec((1, tn), lambda i, j: (0, j)) for _ in extras]
    return pl.pallas_call(
        functools.partial(_mm_kernel, epilogue=epilogue),
        out_shape=jax.ShapeDtypeStruct((m, nw), out_dtype),
        grid=(m // tm, nw // tn),
        in_specs=in_specs,
        out_specs=pl.BlockSpec((tm, tn), lambda i, j: (i, j)),
        compiler_params=_cparams(("arbitrary", "arbitrary"), 48),
        name=name,
    )(a, w, *extras)


def _ep_none(acc):
    return acc


def _ep_colscale(acc, cs):
    return acc * cs


def _ep_grouprms(acc, g, *, width, scale):
    outs = []
    for s in range(0, acc.shape[1], width):
        a = acc[:, s:s + width]
        r = a * lax.rsqrt(jnp.mean(a * a, axis=-1, keepdims=True) + EPS) * g[:, s:s + width]
        outs.append(r * scale if scale is not None else r)
    return outs[0] if len(outs) == 1 else jnp.concatenate(outs, axis=1)


def _ep_small(acc, g, bias):
    lane = lax.broadcasted_iota(I32, acc.shape, 1)
    is_k = lane < SM_WIX
    ms = jnp.sum(jnp.where(is_k, acc * acc, 0.0), axis=-1, keepdims=True) * (1.0 / IDX_DIM)
    kix = acc * lax.rsqrt(ms + EPS) * g
    return jnp.where(is_k, kix, jnp.where(lane < SM_IG, acc * (IDX_HEADS ** -0.5), acc + bias))


def _attn_kernel(slopes_ref, q_ref, qix_ref, wix_ref, kix_ref, kv_ref, kvt_ref, wuv_ref, go_ref,
                 o_ref, key_sc, acc_sc, m_sc, l_sc, *, tq, tk, k_sel):
    i = pl.program_id(1)
    q0 = i * tq
    nchunks = (q0 + tq + tk - 1) // tk
    s_loc = lax.broadcasted_iota(I32, (tk, tq), 0)
    t_abs = q0 + lax.broadcasted_iota(I32, (tk, tq), 1)
    nt = (((1,), (1,)), ((), ()))

    def score_chunk(c, carry):
        off = pl.multiple_of(c * tk, tk)
        kc = kix_ref[pl.ds(off, tk), :]
        acc = jnp.zeros((tk, tq), F32)
        for h in range(IDX_HEADS):
            rel = lax.dot_general(kc, qix_ref[:, h * IDX_PAD:(h + 1) * IDX_PAD], nt, preferred_element_type=F32)
            acc = acc + jnp.maximum(rel, 0.0) * wix_ref[h:h + 1, :]
        bits = pltpu.bitcast(acc, I32)
        key = bits ^ ((bits >> 31) & 0x7FFFFFFF)
        key_sc[pl.ds(off, tk), :] = jnp.where(off + s_loc <= t_abs, key, INT_MIN)
        return carry

    lax.fori_loop(0, nchunks, score_chunk, 0)

    def count_ge(cand):
        def body(c, cnt):
            off = pl.multiple_of(c * tk, tk)
            ge = (key_sc[pl.ds(off, tk), :] >= cand).astype(I32)
            return cnt + jnp.sum(ge, axis=0, keepdims=True)
        return lax.fori_loop(0, nchunks, body, jnp.zeros((1, tq), I32))

    def bisect(it, t_u):
        cand_u = t_u | (jnp.int32(1) << (31 - it))
        cnt = count_ge(cand_u ^ INT_MIN)
        return jnp.where(cnt >= k_sel, cand_u, t_u)

    t_u = lax.fori_loop(0, 32, bisect, jnp.zeros((1, tq), I32))
    thr = jnp.maximum(t_u ^ INT_MIN, INT_MIN + 1)

    m_sc[...] = jnp.full(m_sc.shape, NEG_BIG, F32)
    l_sc[...] = jnp.zeros(l_sc.shape, F32)
    acc_sc[...] = jnp.zeros(acc_sc.shape, F32)

    def att_chunk(c, carry):
        off = pl.multiple_of(c * tk, tk)
        madd = jnp.where(key_sc[pl.ds(off, tk), :] >= thr, 0.0, NEG_BIG)
        dist = (t_abs - (off + s_loc)).astype(F32)
        kc = kv_ref[pl.ds(off, tk), :]
        kct = kvt_ref[:, pl.ds(off, tk)]
        for h in range(ATT_HEADS):
            lg = lax.dot_general(kc, q_ref[:, h * ATT_LAT:(h + 1) * ATT_LAT], nt, preferred_element_type=F32)
            lg = lg - slopes_ref[h] * dist + madd
            m_old = m_sc[h]
            m_new = jnp.maximum(m_old, jnp.max(lg, axis=0, keepdims=True))
            alpha = jnp.exp(m_old - m_new)
            p = jnp.exp(lg - m_new)
            l_sc[h] = alpha * l_sc[h] + jnp.sum(p, axis=0, keepdims=True)
            m_sc[h] = m_new
            acc_sc[h] = alpha * acc_sc[h] + jnp.dot(kct, p.astype(BF16), preferred_element_type=F32)
        return carry

    lax.fori_loop(0, nchunks, att_chunk, 0)

    for h in range(ATT_HEADS):
        o_lat = (acc_sc[h] * (1.0 / l_sc[h])).T
        a = jnp.dot(o_lat.astype(BF16), wuv_ref[h], preferred_element_type=F32)
        r = a * lax.rsqrt(jnp.mean(a * a, axis=-1, keepdims=True) + EPS)
        r = r * go_ref[:, h * ATT_VDIM:(h + 1) * ATT_VDIM]
        o_ref[:, h * ATT_VDIM:(h + 1) * ATT_VDIM] = r.astype(o_ref.dtype)


def _attn_call(slopes, q, qix, qix_col, wixt, kix, kv, kvt, wuv, go, batch, seq):
    tq = _pick(seq, 256)
    tk = tq
    k_sel = min(TOPK_MAX, seq // 4)
    nq = seq // tq
    kern = functools.partial(_attn_kernel, tq=tq, tk=tk, k_sel=k_sel)
    return pl.pallas_call(
        kern,
        out_shape=jax.ShapeDtypeStruct((batch * seq, ATT_WIDTH), BF16),
        grid=(batch, nq),
        in_specs=[pl.BlockSpec(memory_space=pltpu.SMEM),
                  pl.BlockSpec((tq, ATT_HEADS * ATT_LAT), lambda b, i: (b * nq + i, 0)),
                  pl.BlockSpec((tq, IDX_HEADS * IDX_PAD), lambda b, i: (b * nq + i, qix_col)),
                  pl.BlockSpec((None, IDX_HEADS, tq), lambda b, i: (b, 0, i)),
                  pl.BlockSpec((seq, IDX_PAD), lambda b, i: (b, 0)),
                  pl.BlockSpec((seq, ATT_LAT), lambda b, i: (b, 0)),
                  pl.BlockSpec((None, ATT_LAT, seq), lambda b, i: (b, 0, 0)),
                  pl.BlockSpec((ATT_HEADS, ATT_LAT, ATT_VDIM), lambda b, i: (0, 0, 0)),
                  pl.BlockSpec((1, ATT_WIDTH), lambda b, i: (0, 0))],
        out_specs=pl.BlockSpec((tq, ATT_WIDTH), lambda b, i: (b * nq + i, 0)),
        scratch_shapes=[pltpu.VMEM((seq, tq), I32),
                        pltpu.VMEM((ATT_HEADS, ATT_LAT, tq), F32),
                        pltpu.VMEM((ATT_HEADS, 1, tq), F32),
                        pltpu.VMEM((ATT_HEADS, 1, tq), F32)],
        compiler_params=_cparams(("arbitrary", "arbitrary"), 48),
        name="dsa_attention",
    )(slopes, q, qix, wixt, kix, kv, kvt, wuv, go)


def _log_sigmoid(x):
    return jnp.minimum(x, 0.0) - jnp.log1p(jnp.exp(-jnp.abs(x)))


def _mlstm_kernel(q_ref, k_ref, v_ref, og_ref, gcol_ref, grow_ref, g_ref, o_ref, c_sc, n_sc, m_sc, *, lc):
    @pl.when(pl.program_id(1) == 0)
    def _():
        c_sc[...] = jnp.zeros(c_sc.shape, F32)
        n_sc[...] = jnp.zeros(n_sc.shape, F32)
        m_sc[...] = jnp.zeros(m_sc.shape, F32)

    ti = lax.broadcasted_iota(I32, (lc, lc), 0)
    si = lax.broadcasted_iota(I32, (lc, lc), 1)
    tril = si <= ti
    for h in range(ML_HEADS):
        ig_col = gcol_ref[:, SM_IG + h:SM_IG + h + 1]
        lf_col = _log_sigmoid(gcol_ref[:, SM_FG + h:SM_FG + h + 1])
        ig_row = grow_ref[h:h + 1, :]
        lf_row = _log_sigmoid(grow_ref[ML_HEADS + h:ML_HEADS + h + 1, :])
        b_col = jnp.sum(jnp.where(tril, lf_row, 0.0), axis=1, keepdims=True)
        b_row = jnp.sum(jnp.where(ti <= si, lf_col, 0.0), axis=0, keepdims=True)
        m_prev = m_sc[h][:, 0:1]
        dmat = jnp.where(tril, b_col - b_row + ig_row, NEG_BIG)
        m_t = jnp.maximum(b_col + m_prev, jnp.max(dmat, axis=1, keepdims=True))
        inter = jnp.exp(b_col + m_prev - m_t)
        qh = q_ref[:, h * ML_QK:(h + 1) * ML_QK]
        kh = k_ref[:, h * ML_QK:(h + 1) * ML_QK]
        vh = v_ref[:, h * ML_V:(h + 1) * ML_V]
        qk = lax.dot_general(qh, kh, (((1,), (1,)), ((), ())), preferred_element_type=F32)
        s = qk * jnp.exp(dmat - m_t)
        c_old = c_sc[h]
        num = inter * jnp.dot(qh, c_old.astype(BF16), preferred_element_type=F32)
        num = num + jnp.dot(s.astype(BF16), vh, preferred_element_type=F32)
        qn = jnp.sum(qh.astype(F32) * n_sc[h], axis=1, keepdims=True)
        den = inter * qn + jnp.sum(s, axis=1, keepdims=True)
        hval = num / jnp.maximum(jnp.abs(den), jnp.exp(-m_t))
        bl = b_col[lc - 1:lc, :]
        g_col = bl - b_col + ig_col
        m_new = jnp.maximum(bl + m_prev, jnp.max(g_col, axis=0, keepdims=True))
        decay = jnp.exp(bl + m_prev - m_new)
        kw = kh.astype(F32) * jnp.exp(g_col - m_new)
        c_sc[h] = decay * c_old + jnp.dot(kw.T.astype(BF16), vh, preferred_element_type=F32)
        n_sc[h] = decay * n_sc[h] + jnp.sum(kw, axis=0, keepdims=True)
        m_sc[h] = jnp.broadcast_to(m_new, (1, 128))
        r = hval * lax.rsqrt(jnp.mean(hval * hval, axis=-1, keepdims=True) + EPS)
        r = r * g_ref[:, h * ML_V:(h + 1) * ML_V] * _sigmoid(og_ref[:, h * ML_V:(h + 1) * ML_V])
        o_ref[:, h * ML_V:(h + 1) * ML_V] = r.astype(o_ref.dtype)


def _mlstm_call(proj, qcol, kcol, vcol, ogate, small, grow, gain, batch, seq):
    lc = _pick(seq, 256)
    nc = seq // lc
    qw = ML_HEADS * ML_QK
    return pl.pallas_call(
        functools.partial(_mlstm_kernel, lc=lc),
        out_shape=jax.ShapeDtypeStruct((batch * seq, ML_WIDTH), BF16),
        grid=(batch, nc),
        in_specs=[pl.BlockSpec((lc, qw), lambda b, c: (b * nc + c, qcol)),
                  pl.BlockSpec((lc, qw), lambda b, c: (b * nc + c, kcol)),
                  pl.BlockSpec((lc, ML_WIDTH), lambda b, c: (b * nc + c, vcol)),
                  pl.BlockSpec((lc, ML_WIDTH), lambda b, c: (b * nc + c, 0)),
                  pl.BlockSpec((lc, SM_WIDTH), lambda b, c: (b * nc + c, 0)),
                  pl.BlockSpec((None, 2 * ML_HEADS, lc), lambda b, c: (b, 0, c)),
                  pl.BlockSpec((1, ML_WIDTH), lambda b, c: (0, 0))],
        out_specs=pl.BlockSpec((lc, ML_WIDTH), lambda b, c: (b * nc + c, 0)),
        scratch_shapes=[pltpu.VMEM((ML_HEADS, ML_QK, ML_V), F32),
                        pltpu.VMEM((ML_HEADS, 1, ML_QK), F32),
                        pltpu.VMEM((ML_HEADS, 1, 128), F32)],
        compiler_params=_cparams(("arbitrary", "arbitrary"), 40),
        name="mlstm",
    )(proj, proj, proj, ogate, small, grow, gain)


def _outproj_kernel(ya_ref, yb_ref, wa_ref, wb_ref, x_ref, g_ref, o_ref):
    acc = jnp.dot(ya_ref[...], wa_ref[...], preferred_element_type=F32)
    acc = acc + jnp.dot(yb_ref[...], wb_ref[...], preferred_element_type=F32)
    o_ref[...] = x_ref[...] + g_ref[...] * acc


def _outproj_call(ya, yb, w, x2, gate, seq):
    n, d = x2.shape
    ka, kb = ya.shape[1], yb.shape[1]
    assert ka == kb
    tm = _pick(seq, 512)
    tn = _pick(d, 512)
    per = seq // tm
    return pl.pallas_call(
        _outproj_kernel,
        out_shape=jax.ShapeDtypeStruct((n, d), F32),
        grid=(n // tm, d // tn),
        in_specs=[pl.BlockSpec((tm, ka), lambda i, j: (i, 0)),
                  pl.BlockSpec((tm, kb), lambda i, j: (i, 0)),
                  pl.BlockSpec((ka, tn), lambda i, j: (0, j)),
                  pl.BlockSpec((kb, tn), lambda i, j: (1, j)),
                  pl.BlockSpec((tm, tn), lambda i, j: (i, j)),
                  pl.BlockSpec((None, 1, tn), lambda i, j: (i // per, 0, j))],
        out_specs=pl.BlockSpec((tm, tn), lambda i, j: (i, j)),
        compiler_params=_cparams(("arbitrary", "arbitrary"), 40),
        name="out_proj",
    )(ya, yb, w, w, x2, gate)


def _router_kernel(h_ref, w_ref, bias_ref, idx_ref, gw_ref):
    logits = jnp.dot(h_ref[...], w_ref[...], preferred_element_type=F32)
    scores = _sigmoid(logits)
    biased = scores + bias_ref[...]
    shape = biased.shape
    lane = lax.broadcasted_iota(I32, shape, 1)
    grp = lane // GROUP_SIZE
    ninf = -jnp.inf

    def first_max(v):
        mx = jnp.max(v, axis=-1, keepdims=True)
        ix = jnp.min(jnp.where(v == mx, lane, N_EXPERTS), axis=-1, keepdims=True)
        return mx, ix

    gscore = jnp.full(shape, ninf, F32)
    for g in range(N_GROUPS):
        vg = jnp.where(grp == g, biased, ninf)
        m1, i1 = first_max(vg)
        m2 = jnp.max(jnp.where(lane == i1, ninf, vg), axis=-1, keepdims=True)
        gscore = jnp.where(lane == g, m1 + m2, gscore)
    emask = jnp.zeros(shape, I32)
    for _ in range(TOPK_GROUPS):
        _, ig = first_max(gscore)
        emask = jnp.where(grp == ig, 1, emask)
        gscore = jnp.where(lane == ig, ninf, gscore)
    masked = jnp.where(emask > 0, biased, ninf)
    idx_out = jnp.zeros(shape, I32)
    w_out = jnp.zeros(shape, F32)
    wsum = jnp.zeros((shape[0], 1), F32)
    for k in range(TOP_K):
        _, ik = first_max(masked)
        sel = lane == ik
        wk = jnp.sum(jnp.where(sel, scores, 0.0), axis=-1, keepdims=True)
        idx_out = jnp.where(lane == k, ik, idx_out)
        w_out = jnp.where(lane == k, wk, w_out)
        wsum = wsum + wk
        masked = jnp.where(sel, ninf, masked)
    idx_ref[...] = idx_out
    gw_ref[...] = w_out / wsum * ROUTED_SCALE


def _router_call(h, w, bias):
    n, d = h.shape
    tm = _pick(n, 512)
    return pl.pallas_call(
        _router_kernel,
        out_shape=(jax.ShapeDtypeStruct((n, N_EXPERTS), I32), jax.ShapeDtypeStruct((n, N_EXPERTS), F32)),
        grid=(n // tm,),
        in_specs=[pl.BlockSpec((tm, d), lambda i: (i, 0)),
                  pl.BlockSpec((d, N_EXPERTS), lambda i: (0, 0)),
                  pl.BlockSpec((1, N_EXPERTS), lambda i: (0, 0))],
        out_specs=(pl.BlockSpec((tm, N_EXPERTS), lambda i: (i, 0)),
                   pl.BlockSpec((tm, N_EXPERTS), lambda i: (i, 0))),
        compiler_params=_cparams(("arbitrary",), 32),
        name="moe_router",
    )(h, w, bias)


def _swiglu(x, wgu, wd):
    de = wd.shape[0]
    gu = jnp.dot(x, wgu, preferred_element_type=F32)
    g, u = gu[:, :de], gu[:, de:]
    hb = (g * _sigmoid(g)) * u
    return jnp.dot(hb.astype(BF16), wd, preferred_element_type=F32)


CAST_ROWS = 512


def _cast_rows(src, dst, col0):
    rows, cols = src.shape
    step = _pick(rows, CAST_ROWS)

    def body(k, carry):
        r = pl.ds(pl.multiple_of(k * step, step), step)
        dst[r, col0:col0 + cols] = src[r, :].astype(dst.dtype)
        return carry

    lax.fori_loop(0, rows // step, body, 0)


def _expert_kernel(be_ref, first_ref, nxt_ref, nused_ref, x_ref, wg_hbm, wu_hbm, wd_hbm, rw_ref, o_ref,
                   sg, su, sd, wgu_sc, wd_sc, sem):
    i = pl.program_id(0)
    de = sg.shape[1]

    def weight_copies(e):
        return (pltpu.make_async_copy(wg_hbm.at[e], sg, sem.at[0]),
                pltpu.make_async_copy(wu_hbm.at[e], su, sem.at[1]),
                pltpu.make_async_copy(wd_hbm.at[e], sd, sem.at[2]))

    @pl.when(i == 0)
    def _():
        for cp in weight_copies(be_ref[0]):
            cp.start()

    live = i < nused_ref[0]

    @pl.when(jnp.logical_and(live, first_ref[i] == 1))
    def _():
        for cp in weight_copies(be_ref[i]):
            cp.wait()
        _cast_rows(sg, wgu_sc, 0)
        _cast_rows(su, wgu_sc, de)
        _cast_rows(sd, wd_sc, 0)

        @pl.when(nxt_ref[i] >= 0)
        def _():
            for cp in weight_copies(nxt_ref[i]):
                cp.start()

    @pl.when(live)
    def _():
        o_ref[...] = _swiglu(x_ref[...], wgu_sc[...], wd_sc[...]) * rw_ref[...]

    @pl.when(jnp.logical_not(live))
    def _():
        o_ref[...] = jnp.zeros(o_ref.shape, o_ref.dtype)


def _expert_call(blk_e, first, nxt, nused, xs, wg, wu, wd, row_w, tm):
    n_rows, d = xs.shape
    de = wg.shape[2]
    grid_spec = pltpu.PrefetchScalarGridSpec(
        num_scalar_prefetch=4,
        grid=(n_rows // tm,),
        in_specs=[pl.BlockSpec((tm, d), lambda i, *_: (i, 0)),
                  pl.BlockSpec(memory_space=pl.ANY),
                  pl.BlockSpec(memory_space=pl.ANY),
                  pl.BlockSpec(memory_space=pl.ANY),
                  pl.BlockSpec((tm, 1), lambda i, *_: (i, 0))],
        out_specs=pl.BlockSpec((tm, d), lambda i, *_: (i, 0)),
        scratch_shapes=[pltpu.VMEM((d, de), F32),
                        pltpu.VMEM((d, de), F32),
                        pltpu.VMEM((de, d), F32),
                        pltpu.VMEM((d, 2 * de), BF16),
                        pltpu.VMEM((de, d), BF16),
                        pltpu.SemaphoreType.DMA((3,))],
    )
    return pl.pallas_call(
        _expert_kernel,
        out_shape=jax.ShapeDtypeStruct((n_rows, d), F32),
        grid_spec=grid_spec,
        compiler_params=_cparams(("arbitrary",), 56),
        name="moe_experts",
    )(blk_e, first, nxt, nused, xs, wg, wu, wd, row_w)


def _final_kernel(h_ref, r_ref, x_ref, g_ref, wgu_ref, wd_ref, o_ref):
    shared = _swiglu(h_ref[...], wgu_ref[...], wd_ref[...])
    o_ref[...] = x_ref[...] + g_ref[...] * (r_ref[...] + shared)


def _final_call(h, routed, x2, gate, wgu, wd, seq):
    n, d = x2.shape
    de = wd.shape[0]
    tm = _pick(seq, 128)
    per = seq // tm
    return pl.pallas_call(
        _final_kernel,
        out_shape=jax.ShapeDtypeStruct((n, d), F32),
        grid=(n // tm,),
        in_specs=[pl.BlockSpec((tm, d), lambda i: (i, 0)),
                  pl.BlockSpec((tm, d), lambda i: (i, 0)),
                  pl.BlockSpec((tm, d), lambda i: (i, 0)),
                  pl.BlockSpec((None, 1, d), lambda i: (i // per, 0, 0)),
                  pl.BlockSpec((d, 2 * de), lambda i: (0, 0)),
                  pl.BlockSpec((de, d), lambda i: (0, 0))],
        out_specs=pl.BlockSpec((tm, d), lambda i: (i, 0)),
        compiler_params=_cparams(("arbitrary",), 48),
        name="shared_final",
    )(h, routed, x2, gate, wgu, wd)


def _mixer(h, w_in, q_norm_g, kv_norm_g, idx_k_norm_g, w_uv, gate_bias, attn_out_g, mlstm_out_g, batch, seq):
    n, d = h.shape
    st = SPLIT_STARTS

    def cols(i):
        return w_in[:, st[i]:st[i] + SPLITS[i]]

    w_q = cols(0).astype(BF16)
    w_kv = cols(1).astype(BF16)
    w_qix = jnp.pad(cols(2).reshape(d, IDX_HEADS, IDX_DIM), ((0, 0), (0, 0), (0, IDX_PAD - IDX_DIM)))
    w_proj = jnp.concatenate([cols(7), w_qix.reshape(d, IDX_HEADS * IDX_PAD), cols(5), cols(6)], axis=1).astype(BF16)
    w_og = cols(8).astype(BF16)
    n_small = SPLITS[3] + SPLITS[4] + SPLITS[9] + SPLITS[10]
    w_small = jnp.concatenate([cols(3), cols(4), cols(9), cols(10), jnp.zeros((d, SM_WIDTH - n_small), F32)],
                              axis=1).astype(BF16)
    qix_w = IDX_HEADS * IDX_PAD
    qw = ML_HEADS * ML_QK
    cscale = jnp.concatenate([jnp.ones((ML_WIDTH,), F32), jnp.full((qix_w,), IDX_DIM ** -0.5, F32),
                              jnp.full((qw,), ML_QK ** -0.5, F32), jnp.ones((qw,), F32)])[None, :]
    g_small = jnp.concatenate([idx_k_norm_g, jnp.zeros((SM_WIDTH - IDX_DIM,), F32)])[None, :]
    b_small = jnp.concatenate([jnp.zeros((SM_IG,), F32), gate_bias, jnp.zeros((SM_WIDTH - SM_IG - 2 * ML_HEADS,), F32)])[None, :]

    q = _mm_call(h, w_q, [jnp.tile(q_norm_g, ATT_HEADS)[None, :]],
                 functools.partial(_ep_grouprms, width=ATT_LAT, scale=ATT_LAT ** -0.5), BF16, 512, 512, "proj_q")
    kv = _mm_call(h, w_kv, [kv_norm_g[None, :]],
                  functools.partial(_ep_grouprms, width=ATT_LAT, scale=None), BF16, 512, 256, "proj_kv")
    proj = _mm_call(h, w_proj, [cscale], _ep_colscale, BF16, 512, 512, "proj_main")
    ogate = _mm_call(h, w_og, [], _ep_none, F32, 512, 512, "proj_ogate")
    small = _mm_call(h, w_small, [g_small, b_small], _ep_small, F32, 512, SM_WIDTH, "proj_small")

    kix = jnp.pad(small[:, :IDX_DIM].astype(BF16), ((0, 0), (0, IDX_PAD - IDX_DIM)))
    wixt = jnp.swapaxes(small[:, SM_WIX:SM_WIX + IDX_HEADS].reshape(batch, seq, IDX_HEADS), 1, 2)
    kvt = jnp.swapaxes(kv.reshape(batch, seq, ATT_LAT), 1, 2)
    grow = jnp.swapaxes(small[:, SM_IG:SM_IG + 2 * ML_HEADS].reshape(batch, seq, 2 * ML_HEADS), 1, 2)
    slopes = jnp.exp2(-8.0 * jnp.arange(1, ATT_HEADS + 1, dtype=F32) / ATT_HEADS)

    assert ML_WIDTH % qix_w == 0 and (ML_WIDTH + qix_w) % qw == 0
    qix_col = ML_WIDTH // qix_w
    q_col = (ML_WIDTH + qix_w) // qw
    att = _attn_call(slopes, q, proj, qix_col, wixt, kix, kv, kvt, w_uv.astype(BF16), attn_out_g[None, :],
                     batch, seq)
    hm = _mlstm_call(proj, q_col, q_col + 1, 0, ogate, small, grow, mlstm_out_g[None, :], batch, seq)
    return att, hm


def _moe_routed(h, w_router, router_bias, w_gate_e, w_up_e, w_down_e):
    n_tok, d = h.shape
    idx, gw = _router_call(h, w_router.astype(BF16), router_bias[None, :])
    eidx, gw = idx[:, :TOP_K], gw[:, :TOP_K]
    tm = 256 if n_tok * TOP_K >= 256 * N_EXPERTS else 8
    n_asg = n_tok * TOP_K
    flat_e = eidx.reshape(-1)
    flat_tok = jnp.repeat(jnp.arange(n_tok, dtype=I32), TOP_K)
    order = jnp.argsort(flat_e)
    se, stok, sw = flat_e[order], flat_tok[order], gw.reshape(-1)[order]
    counts = jnp.bincount(flat_e, length=N_EXPERTS)
    padded = (counts + tm - 1) // tm * tm
    pend = jnp.cumsum(padded)
    pstart = pend - padded
    sstart = jnp.cumsum(counts) - counts
    dest = (pstart[se] + jnp.arange(n_asg, dtype=I32) - sstart[se]).astype(I32)
    nblk = -(-n_asg // tm) + N_EXPERTS
    n_rows = nblk * tm
    row_tok = jnp.full((n_rows,), n_tok, I32).at[dest].set(stok)
    row_w = jnp.zeros((n_rows,), F32).at[dest].set(sw)
    blk_e = jnp.minimum(jnp.searchsorted(pend, jnp.arange(nblk, dtype=I32) * tm, side='right'),
                        N_EXPERTS - 1).astype(I32)
    nused = (pend[-1] // tm).astype(I32).reshape(1)
    blk_i = jnp.arange(nblk, dtype=I32)
    first = jnp.logical_and(blk_i * tm == pstart[blk_e], blk_i < nused[0])
    nxt_first = lax.cummin(jnp.where(first, blk_i, nblk)[::-1])[::-1]
    nxt_blk = jnp.concatenate([nxt_first[1:], jnp.full((1,), nblk, I32)])
    nxt = jnp.where(nxt_blk < nblk, blk_e[jnp.minimum(nxt_blk, nblk - 1)], -1).astype(I32)
    xs = jnp.concatenate([h, jnp.zeros((1, d), h.dtype)], axis=0)[row_tok]
    ys = _expert_call(blk_e, first.astype(I32), nxt, nused, xs, w_gate_e, w_up_e, w_down_e, row_w[:, None], tm)
    pos = jnp.zeros((n_asg,), I32).at[order].set(dest)
    return ys[pos].reshape(n_tok, TOP_K, d).sum(axis=1)


def kernel(x, c, w_ada, b_ada, w_in, q_norm_g, kv_norm_g, idx_k_norm_g, w_uv, mlstm_gate_bias, attn_out_g,
           mlstm_out_g, w_out, w_router, router_bias, w_gate_e, w_up_e, w_down_e, w_gate_s, w_up_s, w_down_s):
    batch, seq, d = x.shape
    n = batch * seq
    x2 = x.reshape(n, d)
    c8 = jnp.pad(c, ((0, -batch % 8), (0, 0)))
    for l in range(w_ada.shape[0]):
        mod = _ada_call(c8, w_ada[l], b_ada[l][None, :])[:batch]
        sh1, sc1, g1, sh2, sc2, g2 = [m[:, None, :] for m in jnp.split(mod, 6, axis=-1)]
        h = _modnorm_call(x2, sc1, sh1, seq)
        att, hm = _mixer(h, w_in[l], q_norm_g[l], kv_norm_g[l], idx_k_norm_g[l], w_uv[l], mlstm_gate_bias[l],
                         attn_out_g[l], mlstm_out_g[l], batch, seq)
        x2 = _outproj_call(att, hm, w_out[l].astype(BF16), x2, g1, seq)
        h = _modnorm_call(x2, sc2, sh2, seq)
        routed = _moe_routed(h, w_router[l], router_bias[l], w_gate_e[l], w_up_e[l], w_down_e[l])
        wgu_s = jnp.concatenate([w_gate_s[l], w_up_s[l]], axis=1).astype(BF16)
        x2 = _final_call(h, routed, x2, g2, wgu_s, w_down_s[l].astype(BF16), seq)
    return x2.reshape(batch, seq, d)
```

```python
import functools

import jax
import jax.numpy as jnp
import numpy as np
from jax import lax
from jax.experimental import pallas as pl
from jax.experimental.pallas import tpu as pltpu

F32 = jnp.float32
BF16 = jnp.bfloat16
I32 = jnp.int32
U32 = jnp.uint32

EPS = 1e-6
ATT_HEADS = 16
ATT_LAT = 256
ATT_VDIM = 128
IDX_HEADS = 16
IDX_DIM = 64
IDX_PAD = 128
TOPK_MAX = 256
ML_HEADS = 4
ML_QK = 256
ML_V = 512
N_EXPERTS = 128
TOP_K = 8
N_GROUPS = 8
TOPK_GROUPS = 4
GROUP_SIZE = N_EXPERTS // N_GROUPS
D_EXPERT = 384
ROUTED_SCALE = 2.5

ATT_WIDTH = ATT_HEADS * ATT_VDIM
ML_WIDTH = ML_HEADS * ML_V
SPLITS = (ATT_HEADS * ATT_LAT, ATT_LAT, IDX_HEADS * IDX_DIM, IDX_DIM, IDX_HEADS,
          ML_HEADS * ML_QK, ML_HEADS * ML_QK, ML_WIDTH, ML_WIDTH, ML_HEADS, ML_HEADS)
SPLIT_STARTS = tuple(int(v) for v in np.cumsum((0,) + SPLITS[:-1]))

SM_KIX = 0
SM_WIX = IDX_DIM
SM_IG = SM_WIX + IDX_HEADS
SM_FG = SM_IG + ML_HEADS
SM_WIDTH = 128

NEG_BIG = -1e30
INT_MIN = int(np.iinfo(np.int32).min)
V7X_VMEM_LIMIT = 56 * 1024 * 1024


def _cparams(sem, vmem_mb=None):
    kw = dict(dimension_semantics=sem)
    if vmem_mb is not None:
        kw["vmem_limit_bytes"] = min(vmem_mb * 1024 * 1024, V7X_VMEM_LIMIT)
    return pltpu.CompilerParams(**kw)


def _sigmoid(x):
    return 1.0 / (1.0 + jnp.exp(-x))


def _pick(n, pref):
    t = min(n, pref)
    while n % t:
        t //= 2
    return t


def _ada_kernel(c_ref, w_ref, b_ref, o_ref):
    c = c_ref[...]
    cond = (c * _sigmoid(c)).astype(BF16)
    o_ref[...] = jnp.dot(cond, w_ref[...].astype(BF16), preferred_element_type=F32) + b_ref[...]


def _ada_call(c8, w, b):
    rows, d = c8.shape
    nout = w.shape[1]
    tn = _pick(nout, 512)
    return pl.pallas_call(
        _ada_kernel,
        out_shape=jax.ShapeDtypeStruct((rows, nout), F32),
        grid=(nout // tn,),
        in_specs=[pl.BlockSpec((rows, d), lambda j: (0, 0)),
                  pl.BlockSpec((d, tn), lambda j: (0, j)),
                  pl.BlockSpec((1, tn), lambda j: (0, j))],
        out_specs=pl.BlockSpec((rows, tn), lambda j: (0, j)),
        compiler_params=_cparams(("arbitrary",), 40),
        name="ada_proj",
    )(c8, w, b)


def _pack_pairs(x):
    half = x.shape[1] // 2
    lo = pltpu.bitcast(x[:, :half].astype(jnp.bfloat16).astype(F32), U32) >> 16
    hi = pltpu.bitcast(x[:, half:].astype(jnp.bfloat16).astype(F32), U32) & jnp.uint32(0xFFFF0000)
    return hi | lo


def _unpack_pairs(u):
    lo = pltpu.bitcast(u << 16, F32)
    hi = pltpu.bitcast(u & jnp.uint32(0xFFFF0000), F32)
    return jnp.concatenate([lo, hi], axis=1).astype(BF16)


def _modnorm_kernel(x_ref, sc_ref, sh_ref, o_ref, *, packed):
    x = x_ref[...]
    r = x * lax.rsqrt(jnp.mean(x * x, axis=-1, keepdims=True) + EPS)
    h = r * (1.0 + sc_ref[...]) + sh_ref[...]
    o_ref[...] = _pack_pairs(h) if packed else h.astype(o_ref.dtype)


def _modnorm_call(x2, sc, sh, seq, packed=False):
    n, d = x2.shape
    tm = _pick(seq, 256)
    per = seq // tm
    dout = d // 2 if packed else d
    return pl.pallas_call(
        functools.partial(_modnorm_kernel, packed=packed),
        out_shape=jax.ShapeDtypeStruct((n, dout), U32 if packed else BF16),
        grid=(n // tm,),
        in_specs=[pl.BlockSpec((tm, d), lambda i: (i, 0)),
                  pl.BlockSpec((None, 1, d), lambda i: (i // per, 0, 0)),
                  pl.BlockSpec((None, 1, d), lambda i: (i // per, 0, 0))],
        out_specs=pl.BlockSpec((tm, dout), lambda i: (i, 0)),
        compiler_params=_cparams(("arbitrary",), 32),
        name="modnorm",
    )(x2, sc, sh)


def _mm_kernel(a_ref, w_ref, *rest, epilogue):
    o_ref = rest[-1]
    acc = jnp.dot(a_ref[...], w_ref[...], preferred_element_type=F32)
    o_ref[...] = epilogue(acc, *[e[...] for e in rest[:-1]]).astype(o_ref.dtype)


def _mm_call(a, w, extras, epilogue, out_dtype, tm, tn, name):
    m, k = a.shape
    nw = w.shape[1]
    tm = _pick(m, tm)
    tn = _pick(nw, tn)
    in_specs = [pl.BlockSpec((tm, k), lambda i, j: (i, 0)),
                pl.BlockSpec((k, tn), lambda i, j: (0, j))]
    in_specs += [pl.BlockSpec((1, tn), lambda i, j: (0, j)) for _ in extras]
    return pl.pallas_call(
        functools.partial(_mm_kernel, epilogue=epilogue),
        out_shape=jax.ShapeDtypeStruct((m, nw), out_dtype),
        grid=(m // tm, nw // tn),
        in_specs=in_specs,
        out_specs=pl.BlockSpec((tm, tn), lambda i, j: (i, j)),
        compiler_params=_cparams(("arbitrary", "arbitrary"), 48),
        name=name,
    )(a, w, *extras)


def _ep_none(acc):
    return acc


def _ep_colscale(acc, cs):
    return acc * cs


def _ep_grouprms(acc, g, *, width, scale):
    outs = []
    for s in range(0, acc.shape[1], width):
        a = acc[:, s:s + width]
        r = a * lax.rsqrt(jnp.mean(a * a, axis=-1, keepdims=True) + EPS) * g[:, s:s + width]
        outs.append(r * scale if scale is not None else r)
    return outs[0] if len(outs) == 1 else jnp.concatenate(outs, axis=1)


def _ep_small(acc, g, bias):
    lane = lax.broadcasted_iota(I32, acc.shape, 1)
    is_k = lane < SM_WIX
    ms = jnp.sum(jnp.where(is_k, acc * acc, 0.0), axis=-1, keepdims=True) * (1.0 / IDX_DIM)
    kix = acc * lax.rsqrt(ms + EPS) * g
    return jnp.where(is_k, kix, jnp.where(lane < SM_IG, acc * (IDX_HEADS ** -0.5), acc + bias))


def _attn_kernel(slopes_ref, q_ref, qix_ref, wix_ref, kix_ref, kv_ref, kvt_ref, wuv_ref, go_ref,
                 o_ref, key_sc, acc_sc, m_sc, l_sc, *, tq, tk, k_sel):
    i = pl.program_id(1)
    q0 = i * tq
    nchunks = (q0 + tq + tk - 1) // tk
    s_loc = lax.broadcasted_iota(I32, (tk, tq), 0)
    t_abs = q0 + lax.broadcasted_iota(I32, (tk, tq), 1)
    nt = (((1,), (1,)), ((), ()))

    def score_chunk(c, carry):
        off = pl.multiple_of(c * tk, tk)
        kc = kix_ref[pl.ds(off, tk), :]
        acc = jnp.zeros((tk, tq), F32)
        for h in range(IDX_HEADS):
            rel = lax.dot_general(kc, qix_ref[:, h * IDX_PAD:(h + 1) * IDX_PAD], nt, preferred_element_type=F32)
            acc = acc + jnp.maximum(rel, 0.0) * wix_ref[h:h + 1, :]
        bits = pltpu.bitcast(acc, I32)
        key = bits ^ ((bits >> 31) & 0x7FFFFFFF)
        key_sc[pl.ds(off, tk), :] = jnp.where(off + s_loc <= t_abs, key, INT_MIN)
        return carry

    lax.fori_loop(0, nchunks, score_chunk, 0)

    def count_ge(cand):
        def body(c, cnt):
            off = pl.multiple_of(c * tk, tk)
            ge = (key_sc[pl.ds(off, tk), :] >= cand).astype(I32)
            return cnt + jnp.sum(ge, axis=0, keepdims=True)
        return lax.fori_loop(0, nchunks, body, jnp.zeros((1, tq), I32))

    def bisect(it, t_u):
        cand_u = t_u | (jnp.int32(1) << (31 - it))
        cnt = count_ge(cand_u ^ INT_MIN)
        return jnp.where(cnt >= k_sel, cand_u, t_u)

    t_u = lax.fori_loop(0, 32, bisect, jnp.zeros((1, tq), I32))
    thr = jnp.maximum(t_u ^ INT_MIN, INT_MIN + 1)

    m_sc[...] = jnp.full(m_sc.shape, NEG_BIG, F32)
    l_sc[...] = jnp.zeros(l_sc.shape, F32)
    acc_sc[...] = jnp.zeros(acc_sc.shape, F32)

    def att_chunk(c, carry):
        off = pl.multiple_of(c * tk, tk)
        madd = jnp.where(key_sc[pl.ds(off, tk), :] >= thr, 0.0, NEG_BIG)
        dist = (t_abs - (off + s_loc)).astype(F32)
        kc = kv_ref[pl.ds(off, tk), :]
        kct = kvt_ref[:, pl.ds(off, tk)]
        for h in range(ATT_HEADS):
            lg = lax.dot_general(kc, q_ref[:, h * ATT_LAT:(h + 1) * ATT_LAT], nt, preferred_element_type=F32)
            lg = lg - slopes_ref[h] * dist + madd
            m_old = m_sc[h]
            m_new = jnp.maximum(m_old, jnp.max(lg, axis=0, keepdims=True))
            alpha = jnp.exp(m_old - m_new)
            p = jnp.exp(lg - m_new)
            l_sc[h] = alpha * l_sc[h] + jnp.sum(p, axis=0, keepdims=True)
            m_sc[h] = m_new
            acc_sc[h] = alpha * acc_sc[h] + jnp.dot(kct, p.astype(BF16), preferred_element_type=F32)
        return carry

    lax.fori_loop(0, nchunks, att_chunk, 0)

    for h in range(ATT_HEADS):
        o_lat = (acc_sc[h] * (1.0 / l_sc[h])).T
        a = jnp.dot(o_lat.astype(BF16), wuv_ref[h], preferred_element_type=F32)
        r = a * lax.rsqrt(jnp.mean(a * a, axis=-1, keepdims=True) + EPS)
        r = r * go_ref[:, h * ATT_VDIM:(h + 1) * ATT_VDIM]
        o_ref[:, h * ATT_VDIM:(h + 1) * ATT_VDIM] = r.astype(o_ref.dtype)


def _attn_call(slopes, q, qix, qix_col, wixt, kix, kv, kvt, wuv, go, batch, seq):
    tq = _pick(seq, 256)
    tk = tq
    k_sel = min(TOPK_MAX, seq // 4)
    nq = seq // tq
    kern = functools.partial(_attn_kernel, tq=tq, tk=tk, k_sel=k_sel)
    return pl.pallas_call(
        kern,
        out_shape=jax.ShapeDtypeStruct((batch * seq, ATT_WIDTH), BF16),
        grid=(batch, nq),
        in_specs=[pl.BlockSpec(memory_space=pltpu.SMEM),
                  pl.BlockSpec((tq, ATT_HEADS * ATT_LAT), lambda b, i: (b * nq + i, 0)),
                  pl.BlockSpec((tq, IDX_HEADS * IDX_PAD), lambda b, i: (b * nq + i, qix_col)),
                  pl.BlockSpec((None, IDX_HEADS, tq), lambda b, i: (b, 0, i)),
                  pl.BlockSpec((seq, IDX_PAD), lambda b, i: (b, 0)),
                  pl.BlockSpec((seq, ATT_LAT), lambda b, i: (b, 0)),
                  pl.BlockSpec((None, ATT_LAT, seq), lambda b, i: (b, 0, 0)),
                  pl.BlockSpec((ATT_HEADS, ATT_LAT, ATT_VDIM), lambda b, i: (0, 0, 0)),
                  pl.BlockSpec((1, ATT_WIDTH), lambda b, i: (0, 0))],
        out_specs=pl.BlockSpec((tq, ATT_WIDTH), lambda b, i: (b * nq + i, 0)),
        scratch_shapes=[pltpu.VMEM((seq, tq), I32),
                        pltpu.VMEM((ATT_HEADS, ATT_LAT, tq), F32),
                        pltpu.VMEM((ATT_HEADS, 1, tq), F32),
                        pltpu.VMEM((ATT_HEADS, 1, tq), F32)],
        compiler_params=_cparams(("arbitrary", "arbitrary"), 48),
        name="dsa_attention",
    )(slopes, q, qix, wixt, kix, kv, kvt, wuv, go)


def _log_sigmoid(x):
    return jnp.minimum(x, 0.0) - jnp.log1p(jnp.exp(-jnp.abs(x)))


def _mlstm_kernel(q_ref, k_ref, v_ref, og_ref, gcol_ref, grow_ref, g_ref, o_ref, c_sc, n_sc, m_sc, *, lc):
    @pl.when(pl.program_id(1) == 0)
    def _():
        c_sc[...] = jnp.zeros(c_sc.shape, F32)
        n_sc[...] = jnp.zeros(n_sc.shape, F32)
        m_sc[...] = jnp.zeros(m_sc.shape, F32)

    ti = lax.broadcasted_iota(I32, (lc, lc), 0)
    si = lax.broadcasted_iota(I32, (lc, lc), 1)
    tril = si <= ti
    for h in range(ML_HEADS):
        ig_col = gcol_ref[:, SM_IG + h:SM_IG + h + 1]
        lf_col = _log_sigmoid(gcol_ref[:, SM_FG + h:SM_FG + h + 1])
        ig_row = grow_ref[h:h + 1, :]
        lf_row = _log_sigmoid(grow_ref[ML_HEADS + h:ML_HEADS + h + 1, :])
        b_col = jnp.sum(jnp.where(tril, lf_row, 0.0), axis=1, keepdims=True)
        b_row = jnp.sum(jnp.where(ti <= si, lf_col, 0.0), axis=0, keepdims=True)
        m_prev = m_sc[h][:, 0:1]
        dmat = jnp.where(tril, b_col - b_row + ig_row, NEG_BIG)
        m_t = jnp.maximum(b_col + m_prev, jnp.max(dmat, axis=1, keepdims=True))
        inter = jnp.exp(b_col + m_prev - m_t)
        qh = q_ref[:, h * ML_QK:(h + 1) * ML_QK]
        kh = k_ref[:, h * ML_QK:(h + 1) * ML_QK]
        vh = v_ref[:, h * ML_V:(h + 1) * ML_V]
        qk = lax.dot_general(qh, kh, (((1,), (1,)), ((), ())), preferred_element_type=F32)
        s = qk * jnp.exp(dmat - m_t)
        c_old = c_sc[h]
        num = inter * jnp.dot(qh, c_old.astype(BF16), preferred_element_type=F32)
        num = num + jnp.dot(s.astype(BF16), vh, preferred_element_type=F32)
        qn = jnp.sum(qh.astype(F32) * n_sc[h], axis=1, keepdims=True)
        den = inter * qn + jnp.sum(s, axis=1, keepdims=True)
        hval = num / jnp.maximum(jnp.abs(den), jnp.exp(-m_t))
        bl = b_col[lc - 1:lc, :]
        g_col = bl - b_col + ig_col
        m_new = jnp.maximum(bl + m_prev, jnp.max(g_col, axis=0, keepdims=True))
        decay = jnp.exp(bl + m_prev - m_new)
        kw = kh.astype(F32) * jnp.exp(g_col - m_new)
        c_sc[h] = decay * c_old + jnp.dot(kw.T.astype(BF16), vh, preferred_element_type=F32)
        n_sc[h] = decay * n_sc[h] + jnp.sum(kw, axis=0, keepdims=True)
        m_sc[h] = jnp.broadcast_to(m_new, (1, 128))
        r = hval * lax.rsqrt(jnp.mean(hval * hval, axis=-1, keepdims=True) + EPS)
        r = r * g_ref[:, h * ML_V:(h + 1) * ML_V] * _sigmoid(og_ref[:, h * ML_V:(h + 1) * ML_V])
        o_ref[:, h * ML_V:(h + 1) * ML_V] = r.astype(o_ref.dtype)


def _mlstm_call(proj, qcol, kcol, vcol, ogate, small, grow, gain, batch, seq):
    lc = _pick(seq, 256)
    nc = seq // lc
    qw = ML_HEADS * ML_QK
    return pl.pallas_call(
        functools.partial(_mlstm_kernel, lc=lc),
        out_shape=jax.ShapeDtypeStruct((batch * seq, ML_WIDTH), BF16),
        grid=(batch, nc),
        in_specs=[pl.BlockSpec((lc, qw), lambda b, c: (b * nc + c, qcol)),
                  pl.BlockSpec((lc, qw), lambda b, c: (b * nc + c, kcol)),
                  pl.BlockSpec((lc, ML_WIDTH), lambda b, c: (b * nc + c, vcol)),
                  pl.BlockSpec((lc, ML_WIDTH), lambda b, c: (b * nc + c, 0)),
                  pl.BlockSpec((lc, SM_WIDTH), lambda b, c: (b * nc + c, 0)),
                  pl.BlockSpec((None, 2 * ML_HEADS, lc), lambda b, c: (b, 0, c)),
                  pl.BlockSpec((1, ML_WIDTH), lambda b, c: (0, 0))],
        out_specs=pl.BlockSpec((lc, ML_WIDTH), lambda b, c: (b * nc + c, 0)),
        scratch_shapes=[pltpu.VMEM((ML_HEADS, ML_QK, ML_V), F32),
                        pltpu.VMEM((ML_HEADS, 1, ML_QK), F32),
                        pltpu.VMEM((ML_HEADS, 1, 128), F32)],
        compiler_params=_cparams(("arbitrary", "arbitrary"), 40),
        name="mlstm",
    )(proj, proj, proj, ogate, small, grow, gain)


def _outproj_kernel(ya_ref, yb_ref, wa_ref, wb_ref, x_ref, g_ref, o_ref):
    acc = jnp.dot(ya_ref[...], wa_ref[...], preferred_element_type=F32)
    acc = acc + jnp.dot(yb_ref[...], wb_ref[...], preferred_element_type=F32)
    o_ref[...] = x_ref[...] + g_ref[...] * acc


def _outproj_call(ya, yb, w, x2, gate, seq):
    n, d = x2.shape
    ka, kb = ya.shape[1], yb.shape[1]
    assert ka == kb
    tm = _pick(seq, 512)
    tn = _pick(d, 512)
    per = seq // tm
    return pl.pallas_call(
        _outproj_kernel,
        out_shape=jax.ShapeDtypeStruct((n, d), F32),
        grid=(n // tm, d // tn),
        in_specs=[pl.BlockSpec((tm, ka), lambda i, j: (i, 0)),
                  pl.BlockSpec((tm, kb), lambda i, j: (i, 0)),
                  pl.BlockSpec((ka, tn), lambda i, j: (0, j)),
                  pl.BlockSpec((kb, tn), lambda i, j: (1, j)),
                  pl.BlockSpec((tm, tn), lambda i, j: (i, j)),
                  pl.BlockSpec((None, 1, tn), lambda i, j: (i // per, 0, j))],
        out_specs=pl.BlockSpec((tm, tn), lambda i, j: (i, j)),
        compiler_params=_cparams(("arbitrary", "arbitrary"), 40),
        name="out_proj",
    )(ya, yb, w, w, x2, gate)


def _router_kernel(h_ref, w_ref, bias_ref, idx_ref, gw_ref, rank_ref, cnt_ref, carry_sc):
    @pl.when(pl.program_id(0) == 0)
    def _():
        carry_sc[...] = jnp.zeros(carry_sc.shape, F32)

    logits = jnp.dot(_unpack_pairs(h_ref[...]), w_ref[...], preferred_element_type=F32)
    scores = _sigmoid(logits)
    biased = scores + bias_ref[...]
    shape = biased.shape
    lane = lax.broadcasted_iota(I32, shape, 1)
    grp = lane // GROUP_SIZE
    ninf = -jnp.inf

    def first_max(v):
        mx = jnp.max(v, axis=-1, keepdims=True)
        ix = jnp.min(jnp.where(v == mx, lane, N_EXPERTS), axis=-1, keepdims=True)
        return mx, ix

    gscore = jnp.full(shape, ninf, F32)
    for g in range(N_GROUPS):
        vg = jnp.where(grp == g, biased, ninf)
        m1, i1 = first_max(vg)
        m2 = jnp.max(jnp.where(lane == i1, ninf, vg), axis=-1, keepdims=True)
        gscore = jnp.where(lane == g, m1 + m2, gscore)
    emask = jnp.zeros(shape, I32)
    for _ in range(TOPK_GROUPS):
        _, ig = first_max(gscore)
        emask = jnp.where(grp == ig, 1, emask)
        gscore = jnp.where(lane == ig, ninf, gscore)
    masked = jnp.where(emask > 0, biased, ninf)
    idx_out = jnp.zeros(shape, I32)
    w_out = jnp.zeros(shape, F32)
    wsum = jnp.zeros((shape[0], 1), F32)
    chosen = jnp.zeros(shape, F32)
    picks = []
    for k in range(TOP_K):
        _, ik = first_max(masked)
        sel = lane == ik
        wk = jnp.sum(jnp.where(sel, scores, 0.0), axis=-1, keepdims=True)
        idx_out = jnp.where(lane == k, ik, idx_out)
        w_out = jnp.where(lane == k, wk, w_out)
        wsum = wsum + wk
        masked = jnp.where(sel, ninf, masked)
        chosen = jnp.where(sel, 1.0, chosen)
        picks.append(sel)
    idx_ref[...] = idx_out
    gw_ref[...] = w_out / wsum * ROUTED_SCALE
    tm = shape[0]
    lower = lax.broadcasted_iota(I32, (tm, tm), 1) < lax.broadcasted_iota(I32, (tm, tm), 0)
    prefix = jnp.dot(jnp.where(lower, 1.0, 0.0).astype(BF16), chosen.astype(BF16), preferred_element_type=F32)
    before = carry_sc[...] + prefix
    rank_out = jnp.zeros(shape, F32)
    for k in range(TOP_K):
        rk = jnp.sum(jnp.where(picks[k], before, 0.0), axis=-1, keepdims=True)
        rank_out = jnp.where(lane == k, rk, rank_out)
    rank_ref[...] = rank_out.astype(I32)
    carry_sc[...] = carry_sc[...] + jnp.sum(chosen, axis=0, keepdims=True)
    cnt_ref[...] = carry_sc[...]


def _router_call(hp, w, bias):
    n = hp.shape[0]
    d = w.shape[0]
    tm = _pick(n, 512)
    tile = pl.BlockSpec((tm, N_EXPERTS), lambda i: (i, 0))
    return pl.pallas_call(
        _router_kernel,
        out_shape=(jax.ShapeDtypeStruct((n, N_EXPERTS), I32), jax.ShapeDtypeStruct((n, N_EXPERTS), F32),
                   jax.ShapeDtypeStruct((n, N_EXPERTS), I32), jax.ShapeDtypeStruct((1, N_EXPERTS), F32)),
        grid=(n // tm,),
        in_specs=[pl.BlockSpec((tm, d // 2), lambda i: (i, 0)),
                  pl.BlockSpec((d, N_EXPERTS), lambda i: (0, 0)),
                  pl.BlockSpec((1, N_EXPERTS), lambda i: (0, 0))],
        out_specs=(tile, tile, tile, pl.BlockSpec((1, N_EXPERTS), lambda i: (0, 0))),
        scratch_shapes=[pltpu.VMEM((1, N_EXPERTS), F32)],
        compiler_params=_cparams(("arbitrary",), 32),
        name="moe_router",
    )(hp, w, bias)


def _swiglu(x, wgu, wd):
    de = wd.shape[0]
    gu = jnp.dot(x, wgu, preferred_element_type=F32)
    g, u = gu[:, :de], gu[:, de:]
    hb = (g * _sigmoid(g)) * u
    return jnp.dot(hb.astype(BF16), wd, preferred_element_type=F32)


CAST_ROWS = 512


def _cast_rows(src, dst, col0):
    rows, cols = src.shape
    step = _pick(rows, CAST_ROWS)

    def body(k, carry):
        r = pl.ds(pl.multiple_of(k * step, step), step)
        dst[r, col0:col0 + cols] = src[r, :].astype(dst.dtype)
        return carry

    lax.fori_loop(0, rows // step, body, 0)


def _expert_kernel(be_ref, first_ref, nxt_ref, nvalid_ref, x_ref, wg_hbm, wu_hbm, wd_hbm, o_ref,
                   sg, su, sd, wgu_sc, wd_sc, sem):
    i = pl.program_id(0)
    de = sg.shape[1]
    nvalid = nvalid_ref[i]

    def weight_copies(e):
        return (pltpu.make_async_copy(wg_hbm.at[e], sg, sem.at[0]),
                pltpu.make_async_copy(wu_hbm.at[e], su, sem.at[1]),
                pltpu.make_async_copy(wd_hbm.at[e], sd, sem.at[2]))

    @pl.when(i == 0)
    def _():
        for cp in weight_copies(be_ref[0]):
            cp.start()

    live = nvalid > 0

    @pl.when(jnp.logical_and(live, first_ref[i] == 1))
    def _():
        for cp in weight_copies(be_ref[i]):
            cp.wait()
        _cast_rows(sg, wgu_sc, 0)
        _cast_rows(su, wgu_sc, de)
        _cast_rows(sd, wd_sc, 0)

        @pl.when(nxt_ref[i] >= 0)
        def _():
            for cp in weight_copies(nxt_ref[i]):
                cp.start()

    @pl.when(live)
    def _():
        row = lax.broadcasted_iota(I32, x_ref.shape, 0)
        x = _unpack_pairs(jnp.where(row < nvalid, x_ref[...], jnp.uint32(0)))
        o_ref[...] = _swiglu(x, wgu_sc[...], wd_sc[...])

    @pl.when(jnp.logical_not(live))
    def _():
        o_ref[...] = jnp.zeros(o_ref.shape, o_ref.dtype)


def _expert_call(blk_e, first, nxt, nvalid, xs, wg, wu, wd, tm):
    n_rows = xs.shape[0]
    d, de = wg.shape[1], wg.shape[2]
    grid_spec = pltpu.PrefetchScalarGridSpec(
        num_scalar_prefetch=4,
        grid=(n_rows // tm,),
        in_specs=[pl.BlockSpec((tm, d // 2), lambda i, *_: (i, 0)),
                  pl.BlockSpec(memory_space=pl.ANY),
                  pl.BlockSpec(memory_space=pl.ANY),
                  pl.BlockSpec(memory_space=pl.ANY)],
        out_specs=pl.BlockSpec((tm, d), lambda i, *_: (i, 0)),
        scratch_shapes=[pltpu.VMEM((d, de), F32),
                        pltpu.VMEM((d, de), F32),
                        pltpu.VMEM((de, d), F32),
                        pltpu.VMEM((d, 2 * de), BF16),
                        pltpu.VMEM((de, d), BF16),
                        pltpu.SemaphoreType.DMA((3,))],
    )
    return pl.pallas_call(
        _expert_kernel,
        out_shape=jax.ShapeDtypeStruct((n_rows, d), F32),
        grid_spec=grid_spec,
        compiler_params=_cparams(("arbitrary",), 56),
        name="moe_experts",
    )(blk_e, first, nxt, nvalid, xs, wg, wu, wd)


IDX_CHUNK = 1024


def _idx_copy(dest_hbm, idx_smem, sem, step, slot, nidx):
    return pltpu.make_async_copy(dest_hbm.at[pl.ds(pl.multiple_of(step * nidx, nidx), nidx)],
                                 idx_smem.at[pl.ds(pl.multiple_of(slot * nidx, nidx), nidx)], sem.at[slot])


def _dispatch_kernel(dest_hbm, h_ref, wgu_ref, wd_ref, xs_hbm, sh_ref, idx_smem, sem_idx, sem_rows, *, tt):
    i = pl.program_id(0)
    slot = lax.rem(i, 2)
    nidx = tt * TOP_K

    @pl.when(i == 0)
    def _():
        _idx_copy(dest_hbm, idx_smem, sem_idx, 0, 0, nidx).start()

    @pl.when(i + 1 < pl.num_programs(0))
    def _():
        _idx_copy(dest_hbm, idx_smem, sem_idx, i + 1, 1 - slot, nidx).start()

    _idx_copy(dest_hbm, idx_smem, sem_idx, i, slot, nidx).wait()

    def row_copy(r, dst_row):
        return pltpu.make_async_copy(h_ref.at[pl.ds(r, 1)], xs_hbm.at[pl.ds(dst_row, 1)], sem_rows)

    def issue(r, carry):
        for k in range(TOP_K):
            row_copy(r, idx_smem[slot * nidx + r * TOP_K + k]).start()
        return carry

    lax.fori_loop(0, tt, issue, 0)
    sh_ref[...] = _swiglu(_unpack_pairs(h_ref[...]), wgu_ref[...], wd_ref[...])

    def drain(r, carry):
        for k in range(TOP_K):
            row_copy(0, 0).wait()
        return carry

    lax.fori_loop(0, tt, drain, 0)


def _dispatch_call(dest, hp, wgu, wd, n_rows):
    n, dh = hp.shape
    d = wd.shape[1]
    de = wd.shape[0]
    tt = _pick(n, 256)
    nidx = tt * TOP_K
    assert nidx % IDX_CHUNK == 0
    return pl.pallas_call(
        functools.partial(_dispatch_kernel, tt=tt),
        out_shape=(jax.ShapeDtypeStruct((n_rows, dh), U32), jax.ShapeDtypeStruct((n, d), F32)),
        grid=(n // tt,),
        in_specs=[pl.BlockSpec(memory_space=pl.ANY),
                  pl.BlockSpec((tt, dh), lambda i: (i, 0)),
                  pl.BlockSpec((d, 2 * de), lambda i: (0, 0)),
                  pl.BlockSpec((de, d), lambda i: (0, 0))],
        out_specs=(pl.BlockSpec(memory_space=pl.ANY), pl.BlockSpec((tt, d), lambda i: (i, 0))),
        scratch_shapes=[pltpu.SMEM((2 * nidx,), I32),
                        pltpu.SemaphoreType.DMA((2,)),
                        pltpu.SemaphoreType.DMA(())],
        compiler_params=_cparams(("arbitrary",), 48),
        name="moe_dispatch",
    )(dest, hp, wgu, wd)


def _final_kernel(dest_hbm, ys_hbm, x_ref, sh_ref, gw_ref, g_ref, o_ref, buf, idx_smem, sem_idx, sem_rows, *, tt):
    i = pl.program_id(0)
    slot = lax.rem(i, 2)
    nidx = tt * TOP_K

    @pl.when(i == 0)
    def _():
        _idx_copy(dest_hbm, idx_smem, sem_idx, 0, 0, nidx).start()

    @pl.when(i + 1 < pl.num_programs(0))
    def _():
        _idx_copy(dest_hbm, idx_smem, sem_idx, i + 1, 1 - slot, nidx).start()

    _idx_copy(dest_hbm, idx_smem, sem_idx, i, slot, nidx).wait()

    def row_copy(r, k, src_row):
        return pltpu.make_async_copy(ys_hbm.at[pl.ds(src_row, 1)], buf.at[k, pl.ds(r, 1)], sem_rows)

    def issue(r, carry):
        for k in range(TOP_K):
            row_copy(r, k, idx_smem[slot * nidx + r * TOP_K + k]).start()
        return carry

    lax.fori_loop(0, tt, issue, 0)

    def drain(r, carry):
        for k in range(TOP_K):
            row_copy(0, k, 0).wait()
        return carry

    lax.fori_loop(0, tt, drain, 0)
    gw = gw_ref[...]
    routed = gw[:, 0:1] * buf[0]
    for k in range(1, TOP_K):
        routed = routed + gw[:, k:k + 1] * buf[k]
    o_ref[...] = x_ref[...] + g_ref[...] * (routed + sh_ref[...])


def _final_call(dest, ys, x2, shared, gw, gate, seq):
    n, d = x2.shape
    tt = _pick(seq, 128)
    per = seq // tt
    nidx = tt * TOP_K
    assert nidx % IDX_CHUNK == 0
    tile = pl.BlockSpec((tt, d), lambda i: (i, 0))
    return pl.pallas_call(
        functools.partial(_final_kernel, tt=tt),
        out_shape=jax.ShapeDtypeStruct((n, d), F32),
        grid=(n // tt,),
        in_specs=[pl.BlockSpec(memory_space=pl.ANY),
                  pl.BlockSpec(memory_space=pl.ANY),
                  tile, tile,
                  pl.BlockSpec((tt, N_EXPERTS), lambda i: (i, 0)),
                  pl.BlockSpec((None, 1, d), lambda i: (i // per, 0, 0))],
        out_specs=tile,
        scratch_shapes=[pltpu.VMEM((TOP_K, tt, d), F32),
                        pltpu.SMEM((2 * nidx,), I32),
                        pltpu.SemaphoreType.DMA((2,)),
                        pltpu.SemaphoreType.DMA(())],
        compiler_params=_cparams(("arbitrary",), 48),
        name="moe_combine_final",
    )(dest, ys, x2, shared, gw, gate)


def _mixer(h, w_in, q_norm_g, kv_norm_g, idx_k_norm_g, w_uv, gate_bias, attn_out_g, mlstm_out_g, batch, seq):
    n, d = h.shape
    st = SPLIT_STARTS

    def cols(i):
        return w_in[:, st[i]:st[i] + SPLITS[i]]

    w_q = cols(0).astype(BF16)
    w_kv = cols(1).astype(BF16)
    w_qix = jnp.pad(cols(2).reshape(d, IDX_HEADS, IDX_DIM), ((0, 0), (0, 0), (0, IDX_PAD - IDX_DIM)))
    w_proj = jnp.concatenate([cols(7), w_qix.reshape(d, IDX_HEADS * IDX_PAD), cols(5), cols(6)], axis=1).astype(BF16)
    w_og = cols(8).astype(BF16)
    n_small = SPLITS[3] + SPLITS[4] + SPLITS[9] + SPLITS[10]
    w_small = jnp.concatenate([cols(3), cols(4), cols(9), cols(10), jnp.zeros((d, SM_WIDTH - n_small), F32)],
                              axis=1).astype(BF16)
    qix_w = IDX_HEADS * IDX_PAD
    qw = ML_HEADS * ML_QK
    cscale = jnp.concatenate([jnp.ones((ML_WIDTH,), F32), jnp.full((qix_w,), IDX_DIM ** -0.5, F32),
                              jnp.full((qw,), ML_QK ** -0.5, F32), jnp.ones((qw,), F32)])[None, :]
    g_small = jnp.concatenate([idx_k_norm_g, jnp.zeros((SM_WIDTH - IDX_DIM,), F32)])[None, :]
    b_small = jnp.concatenate([jnp.zeros((SM_IG,), F32), gate_bias, jnp.zeros((SM_WIDTH - SM_IG - 2 * ML_HEADS,), F32)])[None, :]

    q = _mm_call(h, w_q, [jnp.tile(q_norm_g, ATT_HEADS)[None, :]],
                 functools.partial(_ep_grouprms, width=ATT_LAT, scale=ATT_LAT ** -0.5), BF16, 512, 512, "proj_q")
    kv = _mm_call(h, w_kv, [kv_norm_g[None, :]],
                  functools.partial(_ep_grouprms, width=ATT_LAT, scale=None), BF16, 512, 256, "proj_kv")
    proj = _mm_call(h, w_proj, [cscale], _ep_colscale, BF16, 512, 512, "proj_main")
    ogate = _mm_call(h, w_og, [], _ep_none, F32, 512, 512, "proj_ogate")
    small = _mm_call(h, w_small, [g_small, b_small], _ep_small, F32, 512, SM_WIDTH, "proj_small")

    kix = jnp.pad(small[:, :IDX_DIM].astype(BF16), ((0, 0), (0, IDX_PAD - IDX_DIM)))
    wixt = jnp.swapaxes(small[:, SM_WIX:SM_WIX + IDX_HEADS].reshape(batch, seq, IDX_HEADS), 1, 2)
    kvt = jnp.swapaxes(kv.reshape(batch, seq, ATT_LAT), 1, 2)
    grow = jnp.swapaxes(small[:, SM_IG:SM_IG + 2 * ML_HEADS].reshape(batch, seq, 2 * ML_HEADS), 1, 2)
    slopes = jnp.exp2(-8.0 * jnp.arange(1, ATT_HEADS + 1, dtype=F32) / ATT_HEADS)

    assert ML_WIDTH % qix_w == 0 and (ML_WIDTH + qix_w) % qw == 0
    qix_col = ML_WIDTH // qix_w
    q_col = (ML_WIDTH + qix_w) // qw
    att = _attn_call(slopes, q, proj, qix_col, wixt, kix, kv, kvt, w_uv.astype(BF16), attn_out_g[None, :],
                     batch, seq)
    hm = _mlstm_call(proj, q_col, q_col + 1, 0, ogate, small, grow, mlstm_out_g[None, :], batch, seq)
    return att, hm


def _moe(hp, x2, gate, w_router, router_bias, w_gate_e, w_up_e, w_down_e, w_gate_s, w_up_s, w_down_s, seq):
    n_tok = hp.shape[0]
    idx, gw, rank, cnt = _router_call(hp, w_router.astype(BF16), router_bias[None, :])
    n_asg = n_tok * TOP_K
    tm = 256 if n_asg >= 256 * N_EXPERTS else 8
    counts = cnt[0].astype(I32)
    padded = (counts + tm - 1) // tm * tm
    pend = jnp.cumsum(padded)
    pstart = pend - padded
    nblk = -(-n_asg // tm) + N_EXPERTS
    n_rows = nblk * tm
    blk_i = jnp.arange(nblk, dtype=I32)
    blk_e = jnp.minimum(jnp.searchsorted(pend, blk_i * tm, side='right'), N_EXPERTS - 1).astype(I32)
    nvalid = jnp.clip(pstart[blk_e] + counts[blk_e] - blk_i * tm, 0, tm).astype(I32)
    first = jnp.logical_and(blk_i * tm == pstart[blk_e], nvalid > 0)
    nxt_first = lax.cummin(jnp.where(first, blk_i, nblk)[::-1])[::-1]
    nxt_blk = jnp.concatenate([nxt_first[1:], jnp.full((1,), nblk, I32)])
    nxt = jnp.where(nxt_blk < nblk, blk_e[jnp.minimum(nxt_blk, nblk - 1)], -1).astype(I32)
    onehot = idx[:, :TOP_K, None] == jnp.arange(N_EXPERTS, dtype=I32)[None, None, :]
    dest = jnp.sum(jnp.where(onehot, pstart[None, None, :], 0), axis=-1) + rank[:, :TOP_K]
    dest = dest.reshape(-1).astype(I32)
    wgu_s = jnp.concatenate([w_gate_s, w_up_s], axis=1).astype(BF16)
    xs, shared = _dispatch_call(dest, hp, wgu_s, w_down_s.astype(BF16), n_rows)
    ys = _expert_call(blk_e, first.astype(I32), nxt, nvalid, xs, w_gate_e, w_up_e, w_down_e, tm)
    return _final_call(dest, ys, x2, shared, gw, gate, seq)


def kernel(x, c, w_ada, b_ada, w_in, q_norm_g, kv_norm_g, idx_k_norm_g, w_uv, mlstm_gate_bias, attn_out_g,
           mlstm_out_g, w_out, w_router, router_bias, w_gate_e, w_up_e, w_down_e, w_gate_s, w_up_s, w_down_s):
    batch, seq, d = x.shape
    n = batch * seq
    x2 = x.reshape(n, d)
    c8 = jnp.pad(c, ((0, -batch % 8), (0, 0)))
    for l in range(w_ada.shape[0]):
        mod = _ada_call(c8, w_ada[l], b_ada[l][None, :])[:batch]
        sh1, sc1, g1, sh2, sc2, g2 = [m[:, None, :] for m in jnp.split(mod, 6, axis=-1)]
        h = _modnorm_call(x2, sc1, sh1, seq)
        att, hm = _mixer(h, w_in[l], q_norm_g[l], kv_norm_g[l], idx_k_norm_g[l], w_uv[l], mlstm_gate_bias[l],
                         attn_out_g[l], mlstm_out_g[l], batch, seq)
        x2 = _outproj_call(att, hm, w_out[l].astype(BF16), x2, g1, seq)
        hp = _modnorm_call(x2, sc2, sh2, seq, packed=True)
        x2 = _moe(hp, x2, g2, w_router[l], router_bias[l], w_gate_e[l], w_up_e[l], w_down_e[l],
                  w_gate_s[l], w_up_s[l], w_down_s[l], seq)
    return x2.reshape(batch, seq, d)
```
